```python
import math
import jax, jax.numpy as jnp
from jax import lax
import numpy as np

D_MODEL = 2048
BATCH = 8
SEQ = 4096
DEPTH = 4

CHUNK = 64
EPS = 1e-6
M_HEADS = 8
M_DV = D_MODEL // 2 // M_HEADS
M_DK = M_DV // 2
M_CONV = 4
G_HEADS = 4
G_DV = D_MODEL // 2 // G_HEADS
G_DK = G_DV // 2
G_RANK = 16
G_TAU = 16.0
S5_GROUP = 16
S5_GROUPS = D_MODEL // S5_GROUP
S5_STATE = 64
D_FF = ((8 * D_MODEL) // 3 + 127) // 128 * 128
FFN_CONV = 3

M_QK = M_HEADS * M_DK
M_V = M_HEADS * M_DV
G_QK = G_HEADS * G_DK
G_V = G_HEADS * G_DV
SPLITS = (M_QK, M_QK, M_V, M_V, M_HEADS, M_HEADS, G_QK, G_QK, G_V, G_V, G_RANK)
SPLIT_IDX = tuple(np.cumsum(SPLITS)[:-1].tolist())
IN_WIDTH = sum(SPLITS)
MIX_WIDTH = M_V + G_V
N_EVEN = (DEPTH + 1) // 2
N_ODD = DEPTH // 2

kernel_name = "hybrid_mlstm_gla_s5_streaming_encoder"


def rmsnorm(x, g):
    xf = x.astype(jnp.float32)
    y = xf * lax.rsqrt(jnp.mean(xf * xf, axis=-1, keepdims=True) + EPS)
    return (y * g.astype(jnp.float32)).astype(x.dtype)


def head_rmsnorm(h, g):
    B, L, H, d = h.shape
    y = h * lax.rsqrt(jnp.mean(h * h, axis=-1, keepdims=True) + EPS)
    return y.reshape(B, L, H * d) * g.astype(jnp.float32)


def causal_dwconv(x, w, b):
    K, C = w.shape
    y = lax.conv_general_dilated(
        x, w[:, None, :].astype(x.dtype), window_strides=(1,), padding=[(K - 1, 0)],
        dimension_numbers=("NWC", "WIO", "NWC"), feature_group_count=C)
    return y + b.astype(x.dtype)


def to_chunks(t):
    B, L = t.shape[:2]
    t = t.reshape((B, L // CHUNK, CHUNK) + t.shape[2:])
    t = jnp.moveaxis(t, 3, 2)
    return jnp.moveaxis(t, 1, 0)


def from_chunks(t):
    nC, B, H, T, d = t.shape
    t = jnp.swapaxes(jnp.moveaxis(t, 0, 1), 2, 3)
    return t.reshape(B, nC * T, H, d)


def mlstm_mixer(q, k, v, i_pre, f_pre):
    B, L, H, DK = q.shape
    DV = v.shape[-1]
    q = q * (DK ** -0.5)
    lf = jax.nn.log_sigmoid(f_pre)
    mask = jnp.tril(jnp.ones((CHUNK, CHUNK), dtype=bool))

    def step(carry, inp):
        C, n, m = carry
        qc, kc, vc, li, lfc = inp
        b = jnp.cumsum(lfc, axis=-1)
        D = jnp.where(mask, b[..., :, None] - b[..., None, :] + li[..., None, :], -jnp.inf)
        inter = b + m[..., None]
        m_out = jnp.maximum(inter, jnp.max(D, axis=-1))
        S = jnp.einsum("bhtd,bhsd->bhts", qc, kc) * jnp.exp(D - m_out[..., None])
        w_inter = jnp.exp(inter - m_out)
        num = (jnp.einsum("bhts,bhsv->bhtv", S, vc)
               + w_inter[..., None] * jnp.einsum("bhtd,bhdv->bhtv", qc, C))
        den = jnp.sum(S, axis=-1) + w_inter * jnp.einsum("bhtd,bhd->bht", qc, n)
        h = num / jnp.maximum(jnp.abs(den), jnp.exp(-m_out))[..., None]
        g = b[..., -1:] - b + li
        m_new = jnp.maximum(b[..., -1] + m, jnp.max(g, axis=-1))
        wk = jnp.exp(g - m_new[..., None])
        decay = jnp.exp(b[..., -1] + m - m_new)
        C_new = decay[..., None, None] * C + jnp.einsum("bhs,bhsd,bhsv->bhdv", wk, kc, vc)
        n_new = decay[..., None] * n + jnp.einsum("bhs,bhsd->bhd", wk, kc)
        return (C_new, n_new, m_new), h

    init = (jnp.zeros((B, H, DK, DV), jnp.float32), jnp.zeros((B, H, DK), jnp.float32),
            jnp.zeros((B, H), jnp.float32))
    _, h = lax.scan(step, init, (to_chunks(q), to_chunks(k), to_chunks(v),
                                 to_chunks(i_pre), to_chunks(lf)))
    return from_chunks(h)


def gla_mixer(q, k, v, log_a):
    B, L, H, DK = q.shape
    DV = v.shape[-1]
    q = q * (DK ** -0.5)
    mask = jnp.tril(jnp.ones((CHUNK, CHUNK), dtype=bool))

    def step(S, inp):
        qc, kc, vc, la = inp
        Bc = jnp.cumsum(la, axis=2)
        rel = jnp.where(mask[:, :, None],
                        Bc[:, :, :, None, :] - Bc[:, :, None, :, :], -jnp.inf)
        A = jnp.einsum("bhtd,bhsd,bhtsd->bhts", qc, kc, jnp.exp(rel))
        o = (jnp.einsum("bhts,bhsv->bhtv", A, vc)
             + jnp.einsum("bhtd,bhdv->bhtv", qc * jnp.exp(Bc), S))
        last = Bc[:, :, -1:, :]
        S_new = (jnp.exp(last[:, :, 0, :])[..., None] * S
                 + jnp.einsum("bhsd,bhsv->bhdv", kc * jnp.exp(last - Bc), vc))
        return S_new, o

    init = jnp.zeros((B, H, DK, DV), jnp.float32)
    _, o = lax.scan(step, init, (to_chunks(q), to_chunks(k), to_chunks(v), to_chunks(log_a)))
    return from_chunks(o)


def even_mixer(u, w_in, m_conv_w, m_conv_b, m_b_igate, m_b_fgate, m_head_norm,
               g_w_gate, g_b_gate, g_head_norm, w_out):
    B, L, _ = u.shape
    f32 = jnp.float32
    proj = u @ w_in
    mq, mk, mv, mo, mi, mf, gq, gk, gv, gg, glr = jnp.split(proj, SPLIT_IDX, axis=-1)
    mqk = jax.nn.silu(causal_dwconv(jnp.concatenate([mq, mk], axis=-1), m_conv_w, m_conv_b))
    mq, mk = jnp.split(mqk.astype(f32), 2, axis=-1)
    hm = mlstm_mixer(mq.reshape(B, L, M_HEADS, M_DK), mk.reshape(B, L, M_HEADS, M_DK),
                     mv.astype(f32).reshape(B, L, M_HEADS, M_DV),
                     (mi + m_b_igate).astype(f32), (mf + m_b_fgate).astype(f32))
    hm = jax.nn.sigmoid(mo.astype(f32)) * head_rmsnorm(hm, m_head_norm)
    log_a = jax.nn.log_sigmoid((glr @ g_w_gate + g_b_gate).astype(f32)) / G_TAU
    hg = gla_mixer(gq.astype(f32).reshape(B, L, G_HEADS, G_DK),
                   gk.astype(f32).reshape(B, L, G_HEADS, G_DK),
                   gv.astype(f32).reshape(B, L, G_HEADS, G_DV),
                   log_a.reshape(B, L, G_HEADS, G_DK))
    hg = jax.nn.silu(gg.astype(f32)) * head_rmsnorm(hg, g_head_norm)
    return jnp.concatenate([hm, hg], axis=-1).astype(u.dtype) @ w_out


def s5_mixer(u, lam_re, lam_im, log_dt, b_re, b_im, c_re, c_im, d, w_glu, b_glu):
    B, L, Dm = u.shape
    f32 = jnp.float32
    nC = L // CHUNK
    uf = u.astype(f32)
    lam = lax.complex(lam_re.astype(f32), lam_im.astype(f32))
    dt = jnp.exp(log_dt.astype(f32))[:, None]
    lam_bar = jnp.exp(lam * dt)
    b = lax.complex(b_re.astype(f32), b_im.astype(f32))
    b_bar = ((lam_bar - 1.0) / lam)[..., None] * b
    c = lax.complex(c_re.astype(f32), c_im.astype(f32))
    steps = jnp.arange(1, CHUNK + 1, dtype=f32)
    pow_t = jnp.exp(lam[None] * dt[None] * steps[:, None, None])
    uc = jnp.moveaxis(uf.reshape(B, nC, CHUNK, S5_GROUPS, S5_GROUP), 1, 0)

    def binop(e1, e2):
        a1, x1 = e1
        a2, x2 = e2
        return a1 * a2, a2 * x1 + x2

    def step(state, u_chunk):
        bu = jnp.einsum("gnp,btgp->btgn", b_bar, u_chunk.astype(jnp.complex64))
        a = jnp.broadcast_to(lam_bar, bu.shape)
        _, xs = lax.associative_scan(binop, (a, bu), axis=1)
        xs = xs + pow_t[None] * state[:, None]
        y = jnp.real(jnp.einsum("gpn,btgn->btgp", c, xs))
        return xs[:, -1], y

    init = jnp.zeros((B, S5_GROUPS, S5_STATE), jnp.complex64)
    _, y = lax.scan(step, init, uc)
    y = jnp.moveaxis(y, 0, 1).reshape(B, L, Dm) + d.astype(f32) * uf
    y = jax.nn.gelu(y)
    y = y * jax.nn.sigmoid(y @ w_glu.astype(f32) + b_glu.astype(f32))
    return y.astype(u.dtype)


def conv_ffn(h, w_up, conv_w, conv_b, w_down):
    a = causal_dwconv(h @ w_up, conv_w, conv_b)
    gate, val = jnp.split(a, 2, axis=-1)
    return (jax.nn.silu(gate) * val) @ w_down


def setup_inputs(seed: int = 0) -> dict:
    key = jax.random.key(seed)
    ks = jax.random.split(key, 32)
    f32 = jnp.float32

    def nrm(k, shape, scale):
        return jax.random.normal(k, shape, f32) * scale

    n_idx = jnp.arange(S5_STATE, dtype=f32)
    return {
        "x": nrm(ks[0], (BATCH, SEQ, D_MODEL), 1.0),
        "norm_mix": 1.0 + nrm(ks[1], (DEPTH, D_MODEL), 0.02),
        "norm_ffn": 1.0 + nrm(ks[2], (DEPTH, D_MODEL), 0.02),
        "ffn_w_up": nrm(ks[3], (DEPTH, D_MODEL, 2 * D_FF), D_MODEL ** -0.5),
        "ffn_conv_w": nrm(ks[4], (DEPTH, FFN_CONV, 2 * D_FF), FFN_CONV ** -0.5),
        "ffn_conv_b": nrm(ks[5], (DEPTH, 2 * D_FF), 0.01),
        "ffn_w_down": nrm(ks[6], (DEPTH, D_FF, D_MODEL), D_FF ** -0.5),
        "norm_final": 1.0 + nrm(ks[7], (D_MODEL,), 0.02),
        "w_in": nrm(ks[8], (N_EVEN, D_MODEL, IN_WIDTH), D_MODEL ** -0.5),
        "m_conv_w": nrm(ks[9], (N_EVEN, M_CONV, 2 * M_QK), M_CONV ** -0.5),
        "m_conv_b": nrm(ks[10], (N_EVEN, 2 * M_QK), 0.01),
        "m_b_igate": nrm(ks[11], (N_EVEN, M_HEADS), 0.1),
        "m_b_fgate": jnp.linspace(3.0, 6.0, M_HEADS, dtype=f32)[None] + nrm(ks[12], (N_EVEN, M_HEADS), 0.1),
        "m_head_norm": 1.0 + nrm(ks[13], (N_EVEN, M_V), 0.02),
        "g_w_gate": nrm(ks[14], (N_EVEN, G_RANK, G_QK), G_RANK ** -0.5),
        "g_b_gate": nrm(ks[15], (N_EVEN, G_QK), 0.1),
        "g_head_norm": 1.0 + nrm(ks[16], (N_EVEN, G_V), 0.02),
        "w_out": nrm(ks[17], (N_EVEN, MIX_WIDTH, D_MODEL), MIX_WIDTH ** -0.5),
        "s5_lambda_re": -0.5 + nrm(ks[18], (N_ODD, S5_GROUPS, S5_STATE), 0.01),
        "s5_lambda_im": math.pi * n_idx + nrm(ks[19], (N_ODD, S5_GROUPS, S5_STATE), 0.01),
        "s5_log_dt": jax.random.uniform(ks[20], (N_ODD, S5_GROUPS), f32, math.log(1e-3), math.log(1e-1)),
        "s5_b_re": nrm(ks[21], (N_ODD, S5_GROUPS, S5_STATE, S5_GROUP), (2 * S5_GROUP) ** -0.5),
        "s5_b_im": nrm(ks[22], (N_ODD, S5_GROUPS, S5_STATE, S5_GROUP), (2 * S5_GROUP) ** -0.5),
        "s5_c_re": nrm(ks[23], (N_ODD, S5_GROUPS, S5_GROUP, S5_STATE), 0.5),
        "s5_c_im": nrm(ks[24], (N_ODD, S5_GROUPS, S5_GROUP, S5_STATE), 0.5),
        "s5_d": nrm(ks[25], (N_ODD, D_MODEL), 1.0),
        "s5_w_glu": nrm(ks[26], (N_ODD, D_MODEL, D_MODEL), D_MODEL ** -0.5),
        "s5_b_glu": nrm(ks[27], (N_ODD, D_MODEL), 0.01),
    }


def reference(x, norm_mix, norm_ffn, ffn_w_up, ffn_conv_w, ffn_conv_b, ffn_w_down, norm_final,
              w_in, m_conv_w, m_conv_b, m_b_igate, m_b_fgate, m_head_norm,
              g_w_gate, g_b_gate, g_head_norm, w_out,
              s5_lambda_re, s5_lambda_im, s5_log_dt, s5_b_re, s5_b_im, s5_c_re, s5_c_im,
              s5_d, s5_w_glu, s5_b_glu):
    h = x
    for layer in range(DEPTH):
        u = rmsnorm(h, norm_mix[layer])
        if layer % 2 == 0:
            e = layer // 2
            h = h + even_mixer(u, w_in[e], m_conv_w[e], m_conv_b[e], m_b_igate[e], m_b_fgate[e],
                               m_head_norm[e], g_w_gate[e], g_b_gate[e], g_head_norm[e], w_out[e])
        else:
            o = layer // 2
            h = h + s5_mixer(u, s5_lambda_re[o], s5_lambda_im[o], s5_log_dt[o], s5_b_re[o],
                             s5_b_im[o], s5_c_re[o], s5_c_im[o], s5_d[o], s5_w_glu[o], s5_b_glu[o])
        h = h + conv_ffn(rmsnorm(h, norm_ffn[layer]), ffn_w_up[layer], ffn_conv_w[layer],
                         ffn_conv_b[layer], ffn_w_down[layer])
    return rmsnorm(h, norm_final)
```

```python
import functools
import math

import jax
import jax.numpy as jnp
from jax import lax
from jax.experimental import pallas as pl
from jax.experimental.pallas import tpu as pltpu

F32 = jnp.float32
BF16 = jnp.bfloat16
HI = lax.Precision.HIGHEST

EPS = 1e-6
CHUNK = 64
M_HEADS, M_DK, M_DV, M_CONV = 8, 64, 128, 4
G_HEADS, G_DK, G_DV, G_RANK, G_TAU = 4, 128, 256, 16, 16.0
S5_P, S5_N = 16, 64
FFN_CONV = 3
GATE_LANES = 128
SUBCHUNK = 16

VMEM_LIMIT = 56 * 1024 * 1024


def _cparams(*sem):
    return pltpu.CompilerParams(dimension_semantics=sem, vmem_limit_bytes=VMEM_LIMIT)


def _rms(x, g):
    return x * lax.rsqrt(jnp.mean(x * x, axis=-1, keepdims=True) + EPS) * g


def _sigmoid(x):
    return 1.0 / (1.0 + jnp.exp(-x))


def _log_sigmoid(x):
    return jnp.minimum(x, 0.0) - jnp.log(1.0 + jnp.exp(-jnp.abs(x)))


def _dot(a, b, **kw):
    return jnp.dot(a, b, preferred_element_type=F32, **kw)


def _dot_nt(a, b, **kw):
    return lax.dot_general(a, b, (((1,), (1,)), ((), ())), preferred_element_type=F32, **kw)


def _dot_tn(a, b, **kw):
    return lax.dot_general(a, b, (((0,), (0,)), ((), ())), preferred_element_type=F32, **kw)


def _inproj_kernel(x_ref, g_ref, w_ref, wg_ref, o_ref, og_ref, xn_ref):
    @pl.when(pl.program_id(1) == 0)
    def _():
        xn = _rms(x_ref[...], g_ref[...]).astype(BF16)
        xn_ref[...] = xn
        og_ref[...] = _dot(xn, wg_ref[...])

    o_ref[...] = _dot(xn_ref[...], w_ref[...]).astype(o_ref.dtype)


def _inproj(h, g, w_main, w_gate, *, tm, tn):
    n, d = h.shape
    wn = w_main.shape[1]
    return pl.pallas_call(
        _inproj_kernel,
        grid=(n // tm, wn // tn),
        in_specs=[
            pl.BlockSpec((tm, d), lambda i, j: (i, 0)),
            pl.BlockSpec((1, d), lambda i, j: (0, 0)),
            pl.BlockSpec((d, tn), lambda i, j: (0, j)),
            pl.BlockSpec((d, GATE_LANES), lambda i, j: (0, 0)),
        ],
        out_specs=[
            pl.BlockSpec((tm, tn), lambda i, j: (i, j)),
            pl.BlockSpec((tm, GATE_LANES), lambda i, j: (i, 0)),
        ],
        out_shape=[
            jax.ShapeDtypeStruct((n, wn), BF16),
            jax.ShapeDtypeStruct((n, GATE_LANES), F32),
        ],
        scratch_shapes=[pltpu.VMEM((tm, d), BF16)],
        compiler_params=_cparams("parallel", "arbitrary"),
        name="inproj",
    )(h, g, w_main, w_gate)


def _mlstm_kernel(mq_ref, mk_ref, mv_ref, mo_ref, gc_ref, gr_ref, cw_ref, cb_ref,
                  brow_ref, bcol_ref, hn_ref, o_ref, qk_scr, qkc_scr, c_scr, m_scr, *, tb):
    @pl.when(pl.program_id(1) == 0)
    def _():
        qk_scr[0:8, :] = jnp.zeros((8, 2 * M_HEADS * M_DK), F32)
        c_scr[...] = jnp.zeros(c_scr.shape, F32)
        m_scr[...] = jnp.zeros(m_scr.shape, F32)

    nqk = M_HEADS * M_DK
    qk_scr[8:8 + tb, 0:nqk] = mq_ref[...].astype(F32)
    qk_scr[8:8 + tb, nqk:2 * nqk] = mk_ref[...].astype(F32)
    conv = cb_ref[...] + cw_ref[0:1, :] * qk_scr[5:5 + tb, :]
    for j in range(1, M_CONV):
        conv = conv + cw_ref[j:j + 1, :] * qk_scr[5 + j:5 + j + tb, :]
    tail = qk_scr[tb:tb + 8, :]
    qkc_scr[...] = conv * _sigmoid(conv)
    qk_scr[0:8, :] = tail

    row = lax.broadcasted_iota(jnp.int32, (CHUNK, CHUNK), 0)
    col = lax.broadcasted_iota(jnp.int32, (CHUNK, CHUNK), 1)
    tril = row >= col
    tri = tril.astype(F32)
    tri_t = (row <= col).astype(F32)
    lane128 = lax.broadcasted_iota(jnp.int32, (1, 2 * M_DK), 1)
    row128 = lax.broadcasted_iota(jnp.int32, (2 * M_DK, 1), 0)
    ones_v = jnp.ones((CHUNK, M_DV), BF16)
    scale = M_DK ** -0.5

    def chunk(c, carry):
        r0 = pl.multiple_of(c * CHUNK, CHUNK)
        rows = pl.ds(r0, CHUNK)
        gcol = gc_ref[rows, :] + brow_ref[...]
        bcum = _dot(tri, _log_sigmoid(gcol), precision=HI)
        grow = gr_ref[0, c] + bcol_ref[...]
        li_rows = grow[0:M_HEADS]
        b_rows = _dot(_log_sigmoid(grow[M_HEADS:2 * M_HEADS]), tri_t, precision=HI)
        for p in range(M_HEADS // 2):
            qp = qkc_scr[rows, 2 * M_DK * p:2 * M_DK * (p + 1)] * scale
            kp = qkc_scr[rows, nqk + 2 * M_DK * p:nqk + 2 * M_DK * (p + 1)]
            kpb = kp.astype(BF16)
            cp = c_scr[p]
            cpb = cp.astype(BF16)
            upd = None
            decays = []
            for e in range(2):
                h = 2 * p + e
                lmask = (lane128 // M_DK) == e
                qm = jnp.where(lmask, qp, 0.0).astype(BF16)
                b_col = bcum[:, M_HEADS + h:M_HEADS + h + 1]
                li_col = gcol[:, h:h + 1]
                b_row = b_rows[h:h + 1, :]
                li_row = li_rows[h:h + 1, :]
                b_last = b_row[:, CHUNK - 1:CHUNK]
                m_prev = m_scr[h][0:1, 0:1]
                dmat = jnp.where(tril, b_col - b_row + li_row, -jnp.inf)
                inter = b_col + m_prev
                m_out = jnp.maximum(inter, jnp.max(dmat, axis=-1, keepdims=True))
                smat = _dot_nt(qm, kpb) * jnp.exp(dmat - m_out)
                w_inter = jnp.exp(inter - m_out)
                vext = jnp.concatenate([mv_ref[rows, M_DV * h:M_DV * (h + 1)], ones_v], axis=-1)
                numext = _dot(smat.astype(BF16), vext) + w_inter * _dot(qm, cpb)
                num = numext[:, 0:M_DV]
                den = numext[:, M_DV:2 * M_DV]
                hh = num / jnp.maximum(jnp.abs(den), jnp.exp(-m_out))
                y = hh * lax.rsqrt(jnp.mean(hh * hh, axis=-1, keepdims=True) + EPS)
                y = y * hn_ref[:, M_DV * h:M_DV * (h + 1)]
                og = _sigmoid(mo_ref[rows, M_DV * h:M_DV * (h + 1)].astype(F32))
                o_ref[rows, M_DV * h:M_DV * (h + 1)] = (og * y).astype(o_ref.dtype)
                g_row = b_last - b_row + li_row
                m_new = jnp.maximum(b_last + m_prev, jnp.max(g_row, axis=-1, keepdims=True))
                wk_col = jnp.exp(b_last - b_col + li_col - m_new)
                decays.append(jnp.exp(b_last + m_prev - m_new))
                kw = jnp.where(lmask, kp * wk_col, 0.0).astype(BF16)
                u = _dot_tn(kw, vext)
                upd = u if upd is None else upd + u
                m_scr[h] = jnp.broadcast_to(m_new, m_scr.shape[1:])
            dcol = jnp.where(row128 < M_DK, decays[0], decays[1])
            c_scr[p] = dcol * cp + upd
        return carry

    lax.fori_loop(0, tb // CHUNK, chunk, 0)


def _mlstm(pm, gates, gates_t, conv_w, conv_b, brow, bcol, hnorm, *, bsz, seq, tb):
    n = pm.shape[0]
    nt = seq // tb
    nqk = M_HEADS * M_DK
    nv = M_HEADS * M_DV
    rowmap = lambda b, t: (b * nt + t, 0)
    return pl.pallas_call(
        functools.partial(_mlstm_kernel, tb=tb),
        grid=(bsz, nt),
        in_specs=[
            pl.BlockSpec((tb, nqk), lambda b, t: (b * nt + t, 0)),
            pl.BlockSpec((tb, nqk), lambda b, t: (b * nt + t, 1)),
            pl.BlockSpec((tb, nv), lambda b, t: (b * nt + t, 1)),
            pl.BlockSpec((tb, nv), lambda b, t: (b * nt + t, 2)),
            pl.BlockSpec((tb, GATE_LANES), rowmap),
            pl.BlockSpec((1, tb // CHUNK, 16, CHUNK), lambda b, t: (b, t, 0, 0)),
            pl.BlockSpec((M_CONV, 2 * nqk), lambda b, t: (0, 0)),
            pl.BlockSpec((1, 2 * nqk), lambda b, t: (0, 0)),
            pl.BlockSpec((1, GATE_LANES), lambda b, t: (0, 0)),
            pl.BlockSpec((16, 1), lambda b, t: (0, 0)),
            pl.BlockSpec((1, nv), lambda b, t: (0, 0)),
        ],
        out_specs=pl.BlockSpec((tb, nv), rowmap),
        out_shape=jax.ShapeDtypeStruct((n, nv), BF16),
        scratch_shapes=[
            pltpu.VMEM((tb + 8, 2 * nqk), F32),
            pltpu.VMEM((tb, 2 * nqk), F32),
            pltpu.VMEM((M_HEADS // 2, 2 * M_DK, 2 * M_DV), F32),
            pltpu.VMEM((M_HEADS, 8, 128), F32),
        ],
        compiler_params=_cparams("parallel", "arbitrary"),
        name="mlstm",
    )(pm, pm, pm, pm, gates, gates_t, conv_w, conv_b, brow, bcol, hnorm)


def _gla_kernel(gq_ref, gk_ref, gv_ref, gg_ref, gc_ref, wg_ref, bg_ref, hn_ref, o_ref, s_scr, *, tb):
    @pl.when(pl.program_id(1) == 0)
    def _():
        s_scr[...] = jnp.zeros(s_scr.shape, F32)

    row = lax.broadcasted_iota(jnp.int32, (CHUNK, CHUNK), 0)
    col = lax.broadcasted_iota(jnp.int32, (CHUNK, CHUNK), 1)
    tril = row >= col
    tri = tril.astype(F32)
    rowk = lax.broadcasted_iota(jnp.int32, (CHUNK, 1), 0)
    nsub = CHUNK // SUBCHUNK
    scale = G_DK ** -0.5

    def chunk(c, carry):
        r0 = pl.multiple_of(c * CHUNK, CHUNK)
        rows = pl.ds(r0, CHUNK)
        pre = _dot(gc_ref[rows, :].astype(BF16), wg_ref[...]) + bg_ref[...]
        la = _log_sigmoid(pre) * (1.0 / G_TAU)
        bc_all = _dot(tri, la, precision=HI)
        for h in range(G_HEADS):
            ks = slice(G_DK * h, G_DK * (h + 1))
            vs = slice(G_DV * h, G_DV * (h + 1))
            bc = bc_all[:, ks]
            q = gq_ref[rows, ks].astype(F32) * scale
            k = gk_ref[rows, ks].astype(F32)
            v = gv_ref[rows, vs]
            st = s_scr[h]
            o = _dot_nt((q * jnp.exp(bc)).astype(BF16), st.astype(BF16))
            cblk = jnp.concatenate(
                [jnp.broadcast_to(bc[SUBCHUNK * i:SUBCHUNK * i + 1, :], (SUBCHUNK, G_DK))
                 for i in range(nsub)], axis=0)
            qt = (q * jnp.exp(bc - cblk)).astype(BF16)
            blocks = []
            for i in range(nsub):
                ci = bc[SUBCHUNK * i:SUBCHUNK * i + 1, :]
                kt = jnp.where(rowk < SUBCHUNK * (i + 1), k * jnp.exp(ci - bc), 0.0).astype(BF16)
                blocks.append(_dot_nt(qt[SUBCHUNK * i:SUBCHUNK * (i + 1), :], kt))
            a = jnp.where(tril, jnp.concatenate(blocks, axis=0), 0.0)
            o = o + _dot(a.astype(BF16), v)
            y = o * lax.rsqrt(jnp.mean(o * o, axis=-1, keepdims=True) + EPS) * hn_ref[:, vs]
            gg = gg_ref[rows, vs].astype(F32)
            o_ref[rows, vs] = (gg * _sigmoid(gg) * y).astype(o_ref.dtype)
            last = bc[CHUNK - 1:CHUNK, :]
            kd = (k * jnp.exp(last - bc)).astype(BF16)
            s_scr[h] = st * jnp.exp(last) + _dot_tn(v, kd)
        return carry

    lax.fori_loop(0, tb // CHUNK, chunk, 0)


def _gla(pm, gates, wg_pad, bg, hnorm, *, bsz, seq, tb):
    n = pm.shape[0]
    nt = seq // tb
    nqk = G_HEADS * G_DK
    nv = G_HEADS * G_DV
    rowmap = lambda b, t: (b * nt + t, 0)
    return pl.pallas_call(
        functools.partial(_gla_kernel, tb=tb),
        grid=(bsz, nt),
        in_specs=[
            pl.BlockSpec((tb, nqk), lambda b, t: (b * nt + t, 6)),
            pl.BlockSpec((tb, nqk), lambda b, t: (b * nt + t, 7)),
            pl.BlockSpec((tb, nv), lambda b, t: (b * nt + t, 4)),
            pl.BlockSpec((tb, nv), lambda b, t: (b * nt + t, 5)),
            pl.BlockSpec((tb, GATE_LANES), rowmap),
            pl.BlockSpec((GATE_LANES, nqk), lambda b, t: (0, 0)),
            pl.BlockSpec((1, nqk), lambda b, t: (0, 0)),
            pl.BlockSpec((1, nv), lambda b, t: (0, 0)),
        ],
        out_specs=pl.BlockSpec((tb, nv), rowmap),
        out_shape=jax.ShapeDtypeStruct((n, nv), BF16),
        scratch_shapes=[pltpu.VMEM((G_HEADS, G_DV, G_DK), F32)],
        compiler_params=_cparams("parallel", "arbitrary"),
        name="gla",
    )(pm, pm, pm, pm, gates, wg_pad, bg, hnorm)


def _outproj_kernel(hm_ref, hg_ref, w1_ref, w2_ref, h_ref, o_ref):
    o_ref[...] = h_ref[...] + _dot(hm_ref[...], w1_ref[...]) + _dot(hg_ref[...], w2_ref[...])


def _outproj(hm, hg, w_out, h, *, tm, tn):
    n, d = h.shape
    kh = hm.shape[1]
    return pl.pallas_call(
        _outproj_kernel,
        grid=(n // tm, d // tn),
        in_specs=[
            pl.BlockSpec((tm, kh), lambda i, j: (i, 0)),
            pl.BlockSpec((tm, kh), lambda i, j: (i, 0)),
            pl.BlockSpec((kh, tn), lambda i, j: (0, j)),
            pl.BlockSpec((kh, tn), lambda i, j: (1, j)),
            pl.BlockSpec((tm, tn), lambda i, j: (i, j)),
        ],
        out_specs=pl.BlockSpec((tm, tn), lambda i, j: (i, j)),
        out_shape=jax.ShapeDtypeStruct((n, d), F32),
        compiler_params=_cparams("parallel", "arbitrary"),
        name="outproj",
    )(hm, hg, w_out, w_out, h)


FFN_HALO = 16


def _ffn_kernel(h_ref, halo_ref, g_ref, wup_ref, cw_ref, cb_ref, wdn_ref, o_ref, xn_ref, a_ref,
                *, tm, tk, rb, tiles_per_seq):
    i = pl.program_id(0)

    @pl.when(pl.program_id(1) == 0)
    def _():
        x = h_ref[...]
        xn_ref[FFN_HALO:, :] = _rms(x, g_ref[...]).astype(BF16)
        xn_ref[0:FFN_HALO, :] = _rms(halo_ref[...], g_ref[...]).astype(BF16)
        o_ref[...] = x

    w = wup_ref[0]
    wd = wdn_ref[0]
    keep = ((i % tiles_per_seq) != 0).astype(F32)
    a_ref[0:FFN_HALO, :] = _dot(xn_ref[0:FFN_HALO, :], w) * keep
    w0 = cw_ref[0, 0:1, :]
    w1 = cw_ref[0, 1:2, :]
    w2 = cw_ref[0, 2:3, :]
    cb = cb_ref[0]
    for r in range(tm // rb):
        r0 = FFN_HALO + r * rb
        a_ref[r0:r0 + rb, :] = _dot(xn_ref[r0:r0 + rb, :], w)
        c = (w2 * a_ref[r0:r0 + rb, :] + w1 * a_ref[r0 - 1:r0 - 1 + rb, :]
             + w0 * a_ref[r0 - 2:r0 - 2 + rb, :] + cb)
        gate = c[:, 0:tk]
        act = (gate * _sigmoid(gate) * c[:, tk:2 * tk]).astype(BF16)
        o_ref[r * rb:(r + 1) * rb, :] += _dot(act, wd)


def _ffn(h, g, wup, cw, cb, wdn, *, seq, tm, rb):
    n, d = h.shape
    nk, _, tk2 = wup.shape
    tk = tk2 // 2
    hb = tm // FFN_HALO
    return pl.pallas_call(
        functools.partial(_ffn_kernel, tm=tm, tk=tk, rb=rb, tiles_per_seq=seq // tm),
        grid=(n // tm, nk),
        in_specs=[
            pl.BlockSpec((tm, d), lambda i, k: (i, 0)),
            pl.BlockSpec((FFN_HALO, d), lambda i, k: (jnp.maximum(i * hb - 1, 0), 0)),
            pl.BlockSpec((1, d), lambda i, k: (0, 0)),
            pl.BlockSpec((1, d, tk2), lambda i, k: (k, 0, 0)),
            pl.BlockSpec((1, FFN_CONV, tk2), lambda i, k: (k, 0, 0)),
            pl.BlockSpec((1, 1, tk2), lambda i, k: (k, 0, 0)),
            pl.BlockSpec((1, tk, d), lambda i, k: (k, 0, 0)),
        ],
        out_specs=pl.BlockSpec((tm, d), lambda i, k: (i, 0)),
        out_shape=jax.ShapeDtypeStruct((n, d), F32),
        scratch_shapes=[
            pltpu.VMEM((tm + FFN_HALO, d), BF16),
            pltpu.VMEM((tm + FFN_HALO, tk2), F32),
        ],
        compiler_params=_cparams("parallel", "arbitrary"),
        name="convffn",
    )(h, h, g, wup, cw, cb, wdn)


def _rmsnorm_kernel(x_ref, g_ref, o_ref):
    o_ref[...] = _rms(x_ref[...], g_ref[...]).astype(o_ref.dtype)


def _rmsnorm(h, g, dtype, *, tm):
    n, d = h.shape
    return pl.pallas_call(
        _rmsnorm_kernel,
        grid=(n // tm,),
        in_specs=[pl.BlockSpec((tm, d), lambda i: (i, 0)), pl.BlockSpec((1, d), lambda i: (0, 0))],
        out_specs=pl.BlockSpec((tm, d), lambda i: (i, 0)),
        out_shape=jax.ShapeDtypeStruct((n, d), dtype),
        compiler_params=_cparams("parallel"),
        name="rmsnorm",
    )(h, g)


S5_TILE = SUBCHUNK * S5_P
S5_W = CHUNK * S5_P


def _s5_gen_kernel(lr_ref, li_ref, lrc_ref, lic_ref, dt_ref, bre_ref, bim_ref, cre_ref, cim_ref,
                   strip_ref, wzr_ref, wzi_ref, ptr_ref, pti_ref, ar_ref, ai_ref):
    dt = jnp.exp(dt_ref[0])
    lam_r = lr_ref[0]
    lam_i = li_ref[0]
    xr = lam_r * dt
    xi = lam_i * dt
    er = jnp.exp(xr)
    lbr = er * jnp.cos(xi)
    lbi = er * jnp.sin(xi)
    den = lam_r * lam_r + lam_i * lam_i
    cfr = ((lbr - 1.0) * lam_r + lbi * lam_i) / den
    cfi = (lbi * lam_r - (lbr - 1.0) * lam_i) / den
    bbr = cfr * bre_ref[0] - cfi * bim_ref[0]
    bbi = cfr * bim_ref[0] + cfi * bre_ref[0]
    kk = lax.broadcasted_iota(jnp.int32, (CHUNK, S5_N), 0).astype(F32)
    pe = jnp.exp(kk * xr)
    pwr = pe * jnp.cos(kk * xi)
    pwi = pe * jnp.sin(kk * xi)
    for s in range(CHUNK):
        pr = pwr[CHUNK - 1 - s:CHUNK - s, :]
        pi = pwi[CHUNK - 1 - s:CHUNK - s, :]
        wzr_ref[0, S5_P * s:S5_P * (s + 1), :] = (bbr * pr - bbi * pi).astype(wzr_ref.dtype)
        wzi_ref[0, S5_P * s:S5_P * (s + 1), :] = (bbr * pi + bbi * pr).astype(wzi_ref.dtype)
    e64 = jnp.exp(CHUNK * xr)
    ar_ref[0] = e64 * jnp.cos(CHUNK * xi)
    ai_ref[0] = e64 * jnp.sin(CHUNK * xi)
    xrc = lrc_ref[0] * dt
    xic = lic_ref[0] * dt
    tt = lax.broadcasted_iota(jnp.int32, (S5_N, CHUNK), 1).astype(F32)
    pte = jnp.exp(tt * xrc)
    ptr = pte * jnp.cos(tt * xic)
    pti = pte * jnp.sin(tt * xic)
    lane = lax.broadcasted_iota(jnp.int32, (CHUNK, S5_W), 1)
    rep_t = ((lane // S5_P) == lax.broadcasted_iota(jnp.int32, (CHUNK, S5_W), 0)).astype(F32)
    lane_p = lax.broadcasted_iota(jnp.int32, (S5_P, S5_W), 1)
    rep_p = ((lane_p % S5_P) == lax.broadcasted_iota(jnp.int32, (S5_P, S5_W), 0)).astype(F32)
    pr_rep = _dot(ptr, rep_t, precision=HI)
    pi_rep = _dot(pti, rep_t, precision=HI)
    cr_rep = _dot(cre_ref[0], rep_p, precision=HI)
    ci_rep = _dot(cim_ref[0], rep_p, precision=HI)
    q0r = cr_rep * pr_rep - ci_rep * pi_rep
    q0i = cr_rep * pi_rep + ci_rep * pr_rep
    erc = jnp.exp(xrc)
    lbrc = erc * jnp.cos(xic)
    lbic = erc * jnp.sin(xic)
    ptr_ref[0] = (q0r * lbrc - q0i * lbic).astype(ptr_ref.dtype)
    pti_ref[0] = (-(q0r * lbic + q0i * lbrc)).astype(pti_ref.dtype)
    kern = _dot(bbr, q0r, precision=HI) - _dot(bbi, q0i, precision=HI)
    lane_w = lax.broadcasted_iota(jnp.int32, (S5_P, S5_W), 1)
    for s in range(SUBCHUNK):
        blk = kern if s == 0 else jnp.where(lane_w >= S5_P * s, pltpu.roll(kern, S5_P * s, axis=1), 0.0)
        strip_ref[0, S5_P * s:S5_P * (s + 1), :] = blk.astype(strip_ref.dtype)


def _s5_gen(lam_re, lam_im, log_dt, b_re_t, b_im_t, c_re_t, c_im_t):
    g = lam_re.shape[0]
    row3 = lambda a: a.reshape(g, 1, -1)
    col3 = lambda a: a.reshape(g, -1, 1)
    blk = lambda s: pl.BlockSpec((1,) + s, lambda i: (i, 0, 0))
    return pl.pallas_call(
        _s5_gen_kernel,
        grid=(g,),
        in_specs=[blk((1, S5_N)), blk((1, S5_N)), blk((S5_N, 1)), blk((S5_N, 1)), blk((1, 1)),
                  blk((S5_P, S5_N)), blk((S5_P, S5_N)), blk((S5_N, S5_P)), blk((S5_N, S5_P))],
        out_specs=[blk((S5_TILE, S5_W)), blk((S5_W, S5_N)), blk((S5_W, S5_N)),
                   blk((S5_N, S5_W)), blk((S5_N, S5_W)), blk((1, S5_N)), blk((1, S5_N))],
        out_shape=[
            jax.ShapeDtypeStruct((g, S5_TILE, S5_W), BF16),
            jax.ShapeDtypeStruct((g, S5_W, S5_N), BF16),
            jax.ShapeDtypeStruct((g, S5_W, S5_N), BF16),
            jax.ShapeDtypeStruct((g, S5_N, S5_W), BF16),
            jax.ShapeDtypeStruct((g, S5_N, S5_W), BF16),
            jax.ShapeDtypeStruct((g, 1, S5_N), F32),
            jax.ShapeDtypeStruct((g, 1, S5_N), F32),
        ],
        compiler_params=_cparams("parallel"),
        name="s5_gen",
    )(row3(lam_re), row3(lam_im), col3(lam_re), col3(lam_im), log_dt.reshape(g, 1, 1),
      b_re_t, b_im_t, c_re_t, c_im_t)


def _s5_apply_kernel(u_ref, strip_ref, wzr_ref, wzi_ref, ptr_ref, pti_ref, ar_ref, ai_ref,
                     y_ref, zr_scr, zi_scr, xr_scr, xi_scr, *, bsz, nchunks):
    u = u_ref[0]
    zr_scr[...] = _dot(u, wzr_ref[0])
    zi_scr[...] = _dot(u, wzi_ref[0])
    a_r = ar_ref[0]
    a_i = ai_ref[0]

    def step(c, carry):
        x_r, x_i = carry
        rows = pl.ds(pl.multiple_of(c * bsz, bsz), bsz)
        xr_scr[rows, :] = x_r
        xi_scr[rows, :] = x_i
        n_r = a_r * x_r - a_i * x_i + zr_scr[rows, :]
        n_i = a_r * x_i + a_i * x_r + zi_scr[rows, :]
        return n_r, n_i

    zero = jnp.zeros((bsz, S5_N), F32)
    lax.fori_loop(0, nchunks, step, (zero, zero))
    xr = xr_scr[...].astype(BF16)
    xi = xi_scr[...].astype(BF16)
    nt = S5_W // S5_TILE
    for j in range(nt):
        cols = slice(S5_TILE * j, S5_TILE * (j + 1))
        acc = _dot(xr, ptr_ref[0, :, cols]) + _dot(xi, pti_ref[0, :, cols])
        for i in range(j + 1):
            acc = acc + _dot(u[:, S5_TILE * i:S5_TILE * (i + 1)],
                             strip_ref[0, :, S5_TILE * (j - i):S5_TILE * (j - i + 1)])
        y_ref[0, :, cols] = acc


def _s5_apply(ut, strip, wzr, wzi, ptr, pti, ar, ai, *, bsz, nchunks):
    g, rows, _ = ut.shape
    blk = lambda s: pl.BlockSpec((1,) + s, lambda i: (i, 0, 0))
    return pl.pallas_call(
        functools.partial(_s5_apply_kernel, bsz=bsz, nchunks=nchunks),
        grid=(g,),
        in_specs=[blk((rows, S5_W)), blk((S5_TILE, S5_W)), blk((S5_W, S5_N)), blk((S5_W, S5_N)),
                  blk((S5_N, S5_W)), blk((S5_N, S5_W)), blk((1, S5_N)), blk((1, S5_N))],
        out_specs=blk((rows, S5_W)),
        out_shape=jax.ShapeDtypeStruct((g, rows, S5_W), F32),
        scratch_shapes=[pltpu.VMEM((rows, S5_N), F32)] * 4,
        compiler_params=_cparams("parallel"),
        name="s5_apply",
    )(ut, strip, wzr, wzi, ptr, pti, ar, ai)


def _s5_glu_kernel(h_ref, y_ref, g_ref, d_ref, w_ref, b_ref, o_ref, yv_ref, yb_ref, *, tn):
    j = pl.program_id(1)

    @pl.when(j == 0)
    def _():
        h = h_ref[...]
        y = y_ref[...] + d_ref[...] * _rms(h, g_ref[...])
        y = 0.5 * y * (1.0 + jnp.tanh(math.sqrt(2.0 / math.pi) * (y + 0.044715 * (y * y * y))))
        yv_ref[...] = y
        yb_ref[...] = y.astype(BF16)

    cols = pl.ds(pl.multiple_of(j * tn, tn), tn)
    z = _dot(yb_ref[...], w_ref[...]) + b_ref[...]
    o_ref[...] = h_ref[:, cols] + yv_ref[:, cols] * _sigmoid(z)


def _s5_glu(h, y, g, dvec, w_glu, b_glu, *, tm, tn):
    n, d = h.shape
    return pl.pallas_call(
        functools.partial(_s5_glu_kernel, tn=tn),
        grid=(n // tm, d // tn),
        in_specs=[
            pl.BlockSpec((tm, d), lambda i, j: (i, 0)),
            pl.BlockSpec((tm, d), lambda i, j: (i, 0)),
            pl.BlockSpec((1, d), lambda i, j: (0, 0)),
            pl.BlockSpec((1, d), lambda i, j: (0, 0)),
            pl.BlockSpec((d, tn), lambda i, j: (0, j)),
            pl.BlockSpec((1, tn), lambda i, j: (0, j)),
        ],
        out_specs=pl.BlockSpec((tm, tn), lambda i, j: (i, j)),
        out_shape=jax.ShapeDtypeStruct((n, d), F32),
        scratch_shapes=[pltpu.VMEM((tm, d), F32), pltpu.VMEM((tm, d), BF16)],
        compiler_params=_cparams("parallel", "arbitrary"),
        name="s5_glu",
    )(h, y, g, dvec, w_glu, b_glu)


def _even_layer(h, g_mix, w_main, w_gate, m_conv_w, m_conv_b, m_b_igate, m_b_fgate, m_head_norm,
                g_w_gate, g_b_gate, g_head_norm, w_out, *, bsz, seq, tm, tb):
    n, d = h.shape
    pm, gates = _inproj(h, g_mix.reshape(1, d), w_main, w_gate, tm=tm, tn=512)
    gates_t = gates[:, 0:16].reshape(bsz, seq // CHUNK, CHUNK, 16).transpose(0, 1, 3, 2)
    bias16 = jnp.concatenate([m_b_igate, m_b_fgate]).astype(F32)
    brow = jnp.zeros((1, GATE_LANES), F32).at[0, 0:16].set(bias16)
    hm = _mlstm(pm, gates, gates_t, m_conv_w, m_conv_b.reshape(1, -1), brow, bias16.reshape(16, 1),
                m_head_norm.reshape(1, -1), bsz=bsz, seq=seq, tb=tb)
    wg_pad = jnp.zeros((GATE_LANES, G_HEADS * G_DK), BF16).at[16:16 + G_RANK].set(g_w_gate.astype(BF16))
    hg = _gla(pm, gates, wg_pad, g_b_gate.reshape(1, -1), g_head_norm.reshape(1, -1),
              bsz=bsz, seq=seq, tb=tb)
    return _outproj(hm, hg, w_out, h, tm=tm, tn=512)


def _odd_layer(h, g_mix, lam_re, lam_im, log_dt, b_re, b_im, c_re, c_im, dvec, w_glu, b_glu,
               *, bsz, seq, tm):
    n, d = h.shape
    groups = d // S5_P
    nchunks = seq // CHUNK
    ops = _s5_gen(lam_re, lam_im, log_dt, jnp.swapaxes(b_re, 1, 2), jnp.swapaxes(b_im, 1, 2),
                  jnp.swapaxes(c_re, 1, 2), jnp.swapaxes(c_im, 1, 2))
    u = _rmsnorm(h, g_mix.reshape(1, d), BF16, tm=tm)
    ut = u.reshape(bsz, nchunks, CHUNK, groups, S5_P).transpose(3, 1, 0, 2, 4)
    ut = ut.reshape(groups, nchunks * bsz, S5_W)
    yt = _s5_apply(ut, *ops, bsz=bsz, nchunks=nchunks)
    y = yt.reshape(groups, nchunks, bsz, CHUNK, S5_P).transpose(2, 1, 3, 0, 4).reshape(n, d)
    return _s5_glu(h, y, g_mix.reshape(1, d), dvec.reshape(1, d), w_glu, b_glu.reshape(1, d),
                   tm=tm, tn=512)


def _prep_ffn(ffn_w_up, ffn_conv_w, ffn_conv_b, ffn_w_down, tk):
    depth, d, two_ff = ffn_w_up.shape
    dff = two_ff // 2
    nk = -(-dff // tk)
    pad = nk * tk - dff

    def tiles(a):
        lead = a.shape[:-1]
        gv = jnp.pad(a.reshape(lead + (2, dff)), [(0, 0)] * len(lead) + [(0, 0), (0, pad)])
        gv = gv.reshape(lead + (2, nk, tk))
        return jnp.moveaxis(gv, -3, -2).reshape(lead + (nk, 2 * tk))

    wup = jnp.moveaxis(tiles(ffn_w_up.astype(BF16)), 2, 1)
    cw = jnp.moveaxis(tiles(ffn_conv_w), 2, 1)
    cb = tiles(ffn_conv_b)[:, :, None, :]
    wdn = jnp.pad(ffn_w_down.astype(BF16), [(0, 0), (0, pad), (0, 0)]).reshape(depth, nk, tk, d)
    return wup, cw, cb, wdn


def kernel(x, norm_mix, norm_ffn, ffn_w_up, ffn_conv_w, ffn_conv_b, ffn_w_down, norm_final,
           w_in, m_conv_w, m_conv_b, m_b_igate, m_b_fgate, m_head_norm,
           g_w_gate, g_b_gate, g_head_norm, w_out,
           s5_lambda_re, s5_lambda_im, s5_log_dt, s5_b_re, s5_b_im, s5_c_re, s5_c_im,
           s5_d, s5_w_glu, s5_b_glu):
    return _forward(x, norm_mix, norm_ffn, ffn_w_up, ffn_conv_w, ffn_conv_b, ffn_w_down, norm_final,
                    w_in, m_conv_w, m_conv_b, m_b_igate, m_b_fgate, m_head_norm,
                    g_w_gate, g_b_gate, g_head_norm, w_out,
                    s5_lambda_re, s5_lambda_im, s5_log_dt, s5_b_re, s5_b_im, s5_c_re, s5_c_im,
                    s5_d, s5_w_glu, s5_b_glu, tm=512, tb=256)


def _forward(x, norm_mix, norm_ffn, ffn_w_up, ffn_conv_w, ffn_conv_b, ffn_w_down, norm_final,
             w_in, m_conv_w, m_conv_b, m_b_igate, m_b_fgate, m_head_norm,
             g_w_gate, g_b_gate, g_head_norm, w_out,
             s5_lambda_re, s5_lambda_im, s5_log_dt, s5_b_re, s5_b_im, s5_c_re, s5_c_im,
             s5_d, s5_w_glu, s5_b_glu, *, tm, tb):
    bsz, seq, d = x.shape
    depth = norm_mix.shape[0]
    n = bsz * seq
    h = x.reshape(n, d)

    wup, cw, cb, wdn = _prep_ffn(ffn_w_up, ffn_conv_w, ffn_conv_b, ffn_w_down, 512)
    c0 = 2 * M_HEADS * M_DK + 2 * M_HEADS * M_DV
    c1 = c0 + 2 * M_HEADS
    c2 = c1 + 2 * G_HEADS * G_DK + 2 * G_HEADS * G_DV
    w_main = jnp.concatenate([w_in[:, :, :c0], w_in[:, :, c1:c2]], axis=-1).astype(BF16)
    w_gate = jnp.concatenate(
        [w_in[:, :, c0:c1], w_in[:, :, c2:],
         jnp.zeros(w_in.shape[:2] + (GATE_LANES - 2 * M_HEADS - G_RANK,), w_in.dtype)], axis=-1).astype(BF16)
    w_out_b = w_out.astype(BF16)
    w_glu_b = s5_w_glu.astype(BF16)

    for layer in range(depth):
        if layer % 2 == 0:
            e = layer // 2
            h = _even_layer(h, norm_mix[layer], w_main[e], w_gate[e], m_conv_w[e], m_conv_b[e],
                            m_b_igate[e], m_b_fgate[e], m_head_norm[e], g_w_gate[e], g_b_gate[e],
                            g_head_norm[e], w_out_b[e], bsz=bsz, seq=seq, tm=tm, tb=tb)
        else:
            o = layer // 2
            h = _odd_layer(h, norm_mix[layer], s5_lambda_re[o], s5_lambda_im[o], s5_log_dt[o],
                           s5_b_re[o], s5_b_im[o], s5_c_re[o], s5_c_im[o], s5_d[o], w_glu_b[o],
                           s5_b_glu[o], bsz=bsz, seq=seq, tm=tm)
        h = _ffn(h, norm_ffn[layer].reshape(1, d), wup[layer], cw[layer], cb[layer], wdn[layer],
                 seq=seq, tm=tm, rb=min(256, tm))
    out = _rmsnorm(h, norm_final.reshape(1, d), F32, tm=tm)
    return out.reshape(bsz, seq, d)
```

```python
import functools
import math

import jax
import jax.numpy as jnp
from jax import lax
from jax.experimental import pallas as pl
from jax.experimental.pallas import tpu as pltpu

F32 = jnp.float32
BF16 = jnp.bfloat16
HI = lax.Precision.HIGHEST

EPS = 1e-6
CHUNK = 64
M_HEADS, M_DK, M_DV, M_CONV = 8, 64, 128, 4
G_HEADS, G_DK, G_DV, G_RANK, G_TAU = 4, 128, 256, 16, 16.0
S5_P, S5_N = 16, 64
FFN_CONV = 3
GATE_LANES = 128
SUBCHUNK = 16

VMEM_LIMIT = 56 * 1024 * 1024


def _cparams(*sem):
    return pltpu.CompilerParams(dimension_semantics=sem, vmem_limit_bytes=VMEM_LIMIT)


def _rms(x, g):
    return x * lax.rsqrt(jnp.mean(x * x, axis=-1, keepdims=True) + EPS) * g


def _sigmoid(x):
    return 1.0 / (1.0 + jnp.exp(-x))


def _log_sigmoid(x):
    return jnp.minimum(x, 0.0) - jnp.log(1.0 + jnp.exp(-jnp.abs(x)))


def _dot(a, b, **kw):
    return jnp.dot(a, b, preferred_element_type=F32, **kw)


def _dot_nt(a, b, **kw):
    return lax.dot_general(a, b, (((1,), (1,)), ((), ())), preferred_element_type=F32, **kw)


def _dot_tn(a, b, **kw):
    return lax.dot_general(a, b, (((0,), (0,)), ((), ())), preferred_element_type=F32, **kw)


def _inproj_kernel(x_ref, g_ref, w_ref, wg_ref, o_ref, og_ref, xn_ref):
    @pl.when(pl.program_id(1) == 0)
    def _():
        _rms_rows_to(xn_ref, 0, x_ref, g_ref, x_ref.shape[0])
        og_ref[...] = _dot(xn_ref[...], wg_ref[...])

    o_ref[...] = _dot(xn_ref[...], w_ref[...]).astype(o_ref.dtype)


def _inproj(h, g, w_main, w_gate, e, *, tm, tn):
    n, d = h.shape
    wn = w_main.shape[2]
    return pl.pallas_call(
        _inproj_kernel,
        grid=(n // tm, wn // tn),
        in_specs=[
            pl.BlockSpec((tm, d), lambda i, j: (i, 0)),
            pl.BlockSpec((1, d), lambda i, j: (0, 0)),
            pl.BlockSpec((None, d, tn), lambda i, j: (e, 0, j)),
            pl.BlockSpec((None, d, GATE_LANES), lambda i, j: (e, 0, 0)),
        ],
        out_specs=[
            pl.BlockSpec((tm, tn), lambda i, j: (i, j)),
            pl.BlockSpec((tm, GATE_LANES), lambda i, j: (i, 0)),
        ],
        out_shape=[
            jax.ShapeDtypeStruct((n, wn), BF16),
            jax.ShapeDtypeStruct((n, GATE_LANES), F32),
        ],
        scratch_shapes=[pltpu.VMEM((tm, d), BF16)],
        compiler_params=_cparams("parallel", "arbitrary"),
        name="inproj",
    )(h, g, w_main, w_gate)


def _mlstm_kernel(mq_ref, mk_ref, mv_ref, mo_ref, gc_ref, gr_ref, cw_ref, cb_ref,
                  brow_ref, bcol_ref, hn_ref, o_ref, qk_scr, qkc_scr, c_scr, m_scr, *, tb):
    @pl.when(pl.program_id(1) == 0)
    def _():
        qk_scr[0:8, :] = jnp.zeros((8, 2 * M_HEADS * M_DK), F32)
        c_scr[...] = jnp.zeros(c_scr.shape, F32)
        m_scr[...] = jnp.zeros(m_scr.shape, F32)

    nqk = M_HEADS * M_DK
    qk_scr[8:8 + tb, 0:nqk] = mq_ref[...].astype(F32)
    qk_scr[8:8 + tb, nqk:2 * nqk] = mk_ref[...].astype(F32)
    conv = cb_ref[...] + cw_ref[0:1, :] * qk_scr[5:5 + tb, :]
    for j in range(1, M_CONV):
        conv = conv + cw_ref[j:j + 1, :] * qk_scr[5 + j:5 + j + tb, :]
    tail = qk_scr[tb:tb + 8, :]
    qkc_scr[...] = conv * _sigmoid(conv)
    qk_scr[0:8, :] = tail

    row = lax.broadcasted_iota(jnp.int32, (CHUNK, CHUNK), 0)
    col = lax.broadcasted_iota(jnp.int32, (CHUNK, CHUNK), 1)
    tril = row >= col
    tri = tril.astype(F32)
    tri_t = (row <= col).astype(F32)
    lane128 = lax.broadcasted_iota(jnp.int32, (1, 2 * M_DK), 1)
    row128 = lax.broadcasted_iota(jnp.int32, (2 * M_DK, 1), 0)
    ones_v = jnp.ones((CHUNK, M_DV), BF16)
    scale = M_DK ** -0.5

    def chunk(c, carry):
        r0 = pl.multiple_of(c * CHUNK, CHUNK)
        rows = pl.ds(r0, CHUNK)
        gcol = gc_ref[rows, :] + brow_ref[...]
        bcum = _dot(tri, _log_sigmoid(gcol), precision=HI)
        grow = gr_ref[0, c] + bcol_ref[...]
        li_rows = grow[0:M_HEADS]
        b_rows = _dot(_log_sigmoid(grow[M_HEADS:2 * M_HEADS]), tri_t, precision=HI)
        for p in range(M_HEADS // 2):
            qp = qkc_scr[rows, 2 * M_DK * p:2 * M_DK * (p + 1)] * scale
            kp = qkc_scr[rows, nqk + 2 * M_DK * p:nqk + 2 * M_DK * (p + 1)]
            kpb = kp.astype(BF16)
            cp = c_scr[p]
            cpb = cp.astype(BF16)
            upd = None
            decays = []
            for e in range(2):
                h = 2 * p + e
                lmask = (lane128 // M_DK) == e
                qm = jnp.where(lmask, qp, 0.0).astype(BF16)
                b_col = bcum[:, M_HEADS + h:M_HEADS + h + 1]
                li_col = gcol[:, h:h + 1]
                b_row = b_rows[h:h + 1, :]
                li_row = li_rows[h:h + 1, :]
                b_last = b_row[:, CHUNK - 1:CHUNK]
                m_prev = m_scr[h][0:1, 0:1]
                dmat = jnp.where(tril, b_col - b_row + li_row, -jnp.inf)
                inter = b_col + m_prev
                m_out = jnp.maximum(inter, jnp.max(dmat, axis=-1, keepdims=True))
                smat = _dot_nt(qm, kpb) * jnp.exp(dmat - m_out)
                w_inter = jnp.exp(inter - m_out)
                vext = jnp.concatenate([mv_ref[rows, M_DV * h:M_DV * (h + 1)], ones_v], axis=-1)
                numext = _dot(smat.astype(BF16), vext) + w_inter * _dot(qm, cpb)
                num = numext[:, 0:M_DV]
                den = numext[:, M_DV:2 * M_DV]
                hh = num / jnp.maximum(jnp.abs(den), jnp.exp(-m_out))
                y = hh * lax.rsqrt(jnp.mean(hh * hh, axis=-1, keepdims=True) + EPS)
                y = y * hn_ref[:, M_DV * h:M_DV * (h + 1)]
                og = _sigmoid(mo_ref[rows, M_DV * h:M_DV * (h + 1)].astype(F32))
                o_ref[rows, M_DV * h:M_DV * (h + 1)] = (og * y).astype(o_ref.dtype)
                g_row = b_last - b_row + li_row
                m_new = jnp.maximum(b_last + m_prev, jnp.max(g_row, axis=-1, keepdims=True))
                wk_col = jnp.exp(b_last - b_col + li_col - m_new)
                decays.append(jnp.exp(b_last + m_prev - m_new))
                kw = jnp.where(lmask, kp * wk_col, 0.0).astype(BF16)
                u = _dot_tn(kw, vext)
                upd = u if upd is None else upd + u
                m_scr[h] = jnp.broadcast_to(m_new, m_scr.shape[1:])
            dcol = jnp.where(row128 < M_DK, decays[0], decays[1])
            c_scr[p] = dcol * cp + upd
        return carry

    lax.fori_loop(0, tb // CHUNK, chunk, 0)


def _mlstm(pm, gates, gates_t, conv_w, conv_b, brow, bcol, hnorm, *, bsz, seq, tb):
    n = pm.shape[0]
    nt = seq // tb
    nqk = M_HEADS * M_DK
    nv = M_HEADS * M_DV
    rowmap = lambda b, t: (b * nt + t, 0)
    return pl.pallas_call(
        functools.partial(_mlstm_kernel, tb=tb),
        grid=(bsz, nt),
        in_specs=[
            pl.BlockSpec((tb, nqk), lambda b, t: (b * nt + t, 0)),
            pl.BlockSpec((tb, nqk), lambda b, t: (b * nt + t, 1)),
            pl.BlockSpec((tb, nv), lambda b, t: (b * nt + t, 1)),
            pl.BlockSpec((tb, nv), lambda b, t: (b * nt + t, 2)),
            pl.BlockSpec((tb, GATE_LANES), rowmap),
            pl.BlockSpec((1, tb // CHUNK, 16, CHUNK), lambda b, t: (b, t, 0, 0)),
            pl.BlockSpec((M_CONV, 2 * nqk), lambda b, t: (0, 0)),
            pl.BlockSpec((1, 2 * nqk), lambda b, t: (0, 0)),
            pl.BlockSpec((1, GATE_LANES), lambda b, t: (0, 0)),
            pl.BlockSpec((16, 1), lambda b, t: (0, 0)),
            pl.BlockSpec((1, nv), lambda b, t: (0, 0)),
        ],
        out_specs=pl.BlockSpec((tb, nv), rowmap),
        out_shape=jax.ShapeDtypeStruct((n, nv), BF16),
        scratch_shapes=[
            pltpu.VMEM((tb + 8, 2 * nqk), F32),
            pltpu.VMEM((tb, 2 * nqk), F32),
            pltpu.VMEM((M_HEADS // 2, 2 * M_DK, 2 * M_DV), F32),
            pltpu.VMEM((M_HEADS, 8, 128), F32),
        ],
        compiler_params=_cparams("parallel", "arbitrary"),
        name="mlstm",
    )(pm, pm, pm, pm, gates, gates_t, conv_w, conv_b, brow, bcol, hnorm)


def _gla_kernel(gq_ref, gk_ref, gv_ref, gg_ref, gc_ref, wg_ref, bg_ref, hn_ref, o_ref, s_scr, *, tb):
    @pl.when(pl.program_id(1) == 0)
    def _():
        s_scr[...] = jnp.zeros(s_scr.shape, F32)

    row = lax.broadcasted_iota(jnp.int32, (CHUNK, CHUNK), 0)
    col = lax.broadcasted_iota(jnp.int32, (CHUNK, CHUNK), 1)
    tril = row >= col
    tri = tril.astype(F32)
    rowk = lax.broadcasted_iota(jnp.int32, (CHUNK, 1), 0)
    nsub = CHUNK // SUBCHUNK
    scale = G_DK ** -0.5

    def chunk(c, carry):
        r0 = pl.multiple_of(c * CHUNK, CHUNK)
        rows = pl.ds(r0, CHUNK)
        pre = _dot(gc_ref[rows, :].astype(BF16), wg_ref[...]) + bg_ref[...]
        la = _log_sigmoid(pre) * (1.0 / G_TAU)
        bc_all = _dot(tri, la, precision=HI)
        for h in range(G_HEADS):
            ks = slice(G_DK * h, G_DK * (h + 1))
            vs = slice(G_DV * h, G_DV * (h + 1))
            bc = bc_all[:, ks]
            q = gq_ref[rows, ks].astype(F32) * scale
            k = gk_ref[rows, ks].astype(F32)
            v = gv_ref[rows, vs]
            st = s_scr[h]
            o = _dot_nt((q * jnp.exp(bc)).astype(BF16), st.astype(BF16))
            cblk = jnp.concatenate(
                [jnp.broadcast_to(bc[SUBCHUNK * i:SUBCHUNK * i + 1, :], (SUBCHUNK, G_DK))
                 for i in range(nsub)], axis=0)
            qt = (q * jnp.exp(bc - cblk)).astype(BF16)
            blocks = []
            for i in range(nsub):
                ci = bc[SUBCHUNK * i:SUBCHUNK * i + 1, :]
                kt = jnp.where(rowk < SUBCHUNK * (i + 1), k * jnp.exp(ci - bc), 0.0).astype(BF16)
                blocks.append(_dot_nt(qt[SUBCHUNK * i:SUBCHUNK * (i + 1), :], kt))
            a = jnp.where(tril, jnp.concatenate(blocks, axis=0), 0.0)
            o = o + _dot(a.astype(BF16), v)
            y = o * lax.rsqrt(jnp.mean(o * o, axis=-1, keepdims=True) + EPS) * hn_ref[:, vs]
            gg = gg_ref[rows, vs].astype(F32)
            o_ref[rows, vs] = (gg * _sigmoid(gg) * y).astype(o_ref.dtype)
            last = bc[CHUNK - 1:CHUNK, :]
            kd = (k * jnp.exp(last - bc)).astype(BF16)
            s_scr[h] = st * jnp.exp(last) + _dot_tn(v, kd)
        return carry

    lax.fori_loop(0, tb // CHUNK, chunk, 0)


def _gla(pm, gates, wg_pad, bg, hnorm, *, bsz, seq, tb):
    n = pm.shape[0]
    nt = seq // tb
    nqk = G_HEADS * G_DK
    nv = G_HEADS * G_DV
    rowmap = lambda b, t: (b * nt + t, 0)
    return pl.pallas_call(
        functools.partial(_gla_kernel, tb=tb),
        grid=(bsz, nt),
        in_specs=[
            pl.BlockSpec((tb, nqk), lambda b, t: (b * nt + t, 6)),
            pl.BlockSpec((tb, nqk), lambda b, t: (b * nt + t, 7)),
            pl.BlockSpec((tb, nv), lambda b, t: (b * nt + t, 4)),
            pl.BlockSpec((tb, nv), lambda b, t: (b * nt + t, 5)),
            pl.BlockSpec((tb, GATE_LANES), rowmap),
            pl.BlockSpec((GATE_LANES, nqk), lambda b, t: (0, 0)),
            pl.BlockSpec((1, nqk), lambda b, t: (0, 0)),
            pl.BlockSpec((1, nv), lambda b, t: (0, 0)),
        ],
        out_specs=pl.BlockSpec((tb, nv), rowmap),
        out_shape=jax.ShapeDtypeStruct((n, nv), BF16),
        scratch_shapes=[pltpu.VMEM((G_HEADS, G_DV, G_DK), F32)],
        compiler_params=_cparams("parallel", "arbitrary"),
        name="gla",
    )(pm, pm, pm, pm, gates, wg_pad, bg, hnorm)


def _outproj_kernel(hm_ref, hg_ref, w1_ref, w2_ref, h_ref, o_ref):
    o_ref[...] = h_ref[...] + _dot(hm_ref[...], w1_ref[...]) + _dot(hg_ref[...], w2_ref[...])


def _outproj(hm, hg, w_out, e, h, *, tm, tn):
    n, d = h.shape
    kh = hm.shape[1]
    return pl.pallas_call(
        _outproj_kernel,
        grid=(n // tm, d // tn),
        in_specs=[
            pl.BlockSpec((tm, kh), lambda i, j: (i, 0)),
            pl.BlockSpec((tm, kh), lambda i, j: (i, 0)),
            pl.BlockSpec((None, kh, tn), lambda i, j: (e, 0, j)),
            pl.BlockSpec((None, kh, tn), lambda i, j: (e, 1, j)),
            pl.BlockSpec((tm, tn), lambda i, j: (i, j)),
        ],
        out_specs=pl.BlockSpec((tm, tn), lambda i, j: (i, j)),
        out_shape=jax.ShapeDtypeStruct((n, d), F32),
        compiler_params=_cparams("parallel", "arbitrary"),
        name="outproj",
    )(hm, hg, w_out, w_out, h)


FFN_HALO = 16
FFN_COLS = 256
NORM_ROWS = 128
GLU_ROWS = 64


def _rms_rows_to(dst_ref, dst_off, src_ref, g_ref, rows):
    g = g_ref[...]
    for r0 in range(0, rows, NORM_ROWS):
        nr = min(NORM_ROWS, rows - r0)
        dst_ref[dst_off + r0:dst_off + r0 + nr, :] = _rms(src_ref[r0:r0 + nr, :], g).astype(dst_ref.dtype)


def _ffn_kernel(h_ref, halo_ref, g_ref, wg_ref, wv_ref, cwg_ref, cwv_ref, cbg_ref, cbv_ref, wdn_ref,
                o_ref, xn_ref, ag_ref, av_ref, act_ref, *, tm, tk, nk, tiles_per_seq):
    i = pl.program_id(0)
    k = pl.program_id(1)
    ncol = tk // FFN_COLS

    @pl.when(k == 0)
    def _():
        _rms_rows_to(xn_ref, 0, halo_ref, g_ref, FFN_HALO)
        _rms_rows_to(xn_ref, FFN_HALO, h_ref, g_ref, tm)
        o_ref[...] = h_ref[...]

    keep = ((i % tiles_per_seq) != 0).astype(F32)

    def up(c):
        cs = slice(FFN_COLS * c, FFN_COLS * (c + 1))
        ag_ref[c] = _dot(xn_ref[...], wg_ref[:, cs])
        av_ref[c] = _dot(xn_ref[...], wv_ref[:, cs])

    def conv(a_ref, c, cw_ref, cb_ref):
        cs = slice(FFN_COLS * c, FFN_COLS * (c + 1))
        a_ref[c, 0:FFN_HALO, :] = a_ref[c, 0:FFN_HALO, :] * keep
        return (cw_ref[2:3, cs] * a_ref[c, FFN_HALO:FFN_HALO + tm, :]
                + cw_ref[1:2, cs] * a_ref[c, FFN_HALO - 1:FFN_HALO - 1 + tm, :]
                + cw_ref[0:1, cs] * a_ref[c, FFN_HALO - 2:FFN_HALO - 2 + tm, :] + cb_ref[:, cs])

    def gating(c, slot):
        gate = conv(ag_ref, c, cwg_ref, cbg_ref)
        val = conv(av_ref, c, cwv_ref, cbv_ref)
        act_ref[slot, :, FFN_COLS * c:FFN_COLS * (c + 1)] = (gate * _sigmoid(gate) * val).astype(BF16)

    def step(slot_w, slot_r):
        if slot_w is not None:
            for c in range(ncol):
                up(c)
        if slot_r is not None:
            o_ref[...] += _dot(act_ref[slot_r], wdn_ref[...])
        if slot_w is not None:
            for c in range(ncol):
                gating(c, slot_w)

    @pl.when(k == 0)
    def _():
        step(0, None)

    for par in range(2):
        @pl.when((k > 0) & (k < nk) & (k % 2 == par))
        def _():
            step(par, 1 - par)

    @pl.when(k == nk)
    def _():
        step(None, (nk - 1) % 2)


def _ffn(h, g, wup, cw, cb, wdn, layer, *, seq, tm, tk):
    n, d = h.shape
    nk = wdn.shape[1] // tk
    hb = tm // FFN_HALO
    upk = lambda k: jnp.minimum(k, nk - 1)
    dnk = lambda k: jnp.maximum(k - 1, 0)
    return pl.pallas_call(
        functools.partial(_ffn_kernel, tm=tm, tk=tk, nk=nk, tiles_per_seq=seq // tm),
        grid=(n // tm, nk + 1),
        in_specs=[
            pl.BlockSpec((tm, d), lambda i, k: (i, 0)),
            pl.BlockSpec((FFN_HALO, d), lambda i, k: (jnp.maximum(i * hb - 1, 0), 0)),
            pl.BlockSpec((1, d), lambda i, k: (0, 0)),
            pl.BlockSpec((None, d, tk), lambda i, k: (layer, 0, upk(k))),
            pl.BlockSpec((None, d, tk), lambda i, k: (layer, 0, nk + upk(k))),
            pl.BlockSpec((None, FFN_CONV, tk), lambda i, k: (layer, 0, upk(k))),
            pl.BlockSpec((None, FFN_CONV, tk), lambda i, k: (layer, 0, nk + upk(k))),
            pl.BlockSpec((None, 1, tk), lambda i, k: (layer, 0, upk(k))),
            pl.BlockSpec((None, 1, tk), lambda i, k: (layer, 0, nk + upk(k))),
            pl.BlockSpec((None, tk, d), lambda i, k: (layer, dnk(k), 0)),
        ],
        out_specs=pl.BlockSpec((tm, d), lambda i, k: (i, 0)),
        out_shape=jax.ShapeDtypeStruct((n, d), F32),
        scratch_shapes=[
            pltpu.VMEM((tm + FFN_HALO, d), BF16),
            pltpu.VMEM((tk // FFN_COLS, tm + FFN_HALO, FFN_COLS), F32),
            pltpu.VMEM((tk // FFN_COLS, tm + FFN_HALO, FFN_COLS), F32),
            pltpu.VMEM((2, tm, tk), BF16),
        ],
        compiler_params=_cparams("parallel", "arbitrary"),
        name="convffn",
    )(h, h, g, wup, wup, cw, cw, cb, cb, wdn)


def _rmsnorm_kernel(x_ref, g_ref, o_ref):
    o_ref[...] = _rms(x_ref[...], g_ref[...]).astype(o_ref.dtype)


def _rmsnorm(h, g, dtype, *, tm):
    n, d = h.shape
    return pl.pallas_call(
        _rmsnorm_kernel,
        grid=(n // tm,),
        in_specs=[pl.BlockSpec((tm, d), lambda i: (i, 0)), pl.BlockSpec((1, d), lambda i: (0, 0))],
        out_specs=pl.BlockSpec((tm, d), lambda i: (i, 0)),
        out_shape=jax.ShapeDtypeStruct((n, d), dtype),
        compiler_params=_cparams("parallel"),
        name="rmsnorm",
    )(h, g)


S5_TILE = SUBCHUNK * S5_P
S5_W = CHUNK * S5_P


def _s5_gen_kernel(lr_ref, li_ref, lrc_ref, lic_ref, dt_ref, bre_ref, bim_ref, cre_ref, cim_ref,
                   strip_ref, wzr_ref, wzi_ref, ptr_ref, pti_ref, ar_ref, ai_ref):
    dt = jnp.exp(dt_ref[0])
    lam_r = lr_ref[0]
    lam_i = li_ref[0]
    xr = lam_r * dt
    xi = lam_i * dt
    er = jnp.exp(xr)
    lbr = er * jnp.cos(xi)
    lbi = er * jnp.sin(xi)
    den = lam_r * lam_r + lam_i * lam_i
    cfr = ((lbr - 1.0) * lam_r + lbi * lam_i) / den
    cfi = (lbi * lam_r - (lbr - 1.0) * lam_i) / den
    bbr = cfr * bre_ref[0] - cfi * bim_ref[0]
    bbi = cfr * bim_ref[0] + cfi * bre_ref[0]
    kk = lax.broadcasted_iota(jnp.int32, (CHUNK, S5_N), 0).astype(F32)
    pe = jnp.exp(kk * xr)
    pwr = pe * jnp.cos(kk * xi)
    pwi = pe * jnp.sin(kk * xi)
    for s in range(CHUNK):
        pr = pwr[CHUNK - 1 - s:CHUNK - s, :]
        pi = pwi[CHUNK - 1 - s:CHUNK - s, :]
        wzr_ref[0, S5_P * s:S5_P * (s + 1), :] = (bbr * pr - bbi * pi).astype(wzr_ref.dtype)
        wzi_ref[0, S5_P * s:S5_P * (s + 1), :] = (bbr * pi + bbi * pr).astype(wzi_ref.dtype)
    e64 = jnp.exp(CHUNK * xr)
    ar_ref[0] = e64 * jnp.cos(CHUNK * xi)
    ai_ref[0] = e64 * jnp.sin(CHUNK * xi)
    xrc = lrc_ref[0] * dt
    xic = lic_ref[0] * dt
    tt = lax.broadcasted_iota(jnp.int32, (S5_N, CHUNK), 1).astype(F32)
    pte = jnp.exp(tt * xrc)
    ptr = pte * jnp.cos(tt * xic)
    pti = pte * jnp.sin(tt * xic)
    lane = lax.broadcasted_iota(jnp.int32, (CHUNK, S5_W), 1)
    rep_t = ((lane // S5_P) == lax.broadcasted_iota(jnp.int32, (CHUNK, S5_W), 0)).astype(F32)
    lane_p = lax.broadcasted_iota(jnp.int32, (S5_P, S5_W), 1)
    rep_p = ((lane_p % S5_P) == lax.broadcasted_iota(jnp.int32, (S5_P, S5_W), 0)).astype(F32)
    pr_rep = _dot(ptr, rep_t, precision=HI)
    pi_rep = _dot(pti, rep_t, precision=HI)
    cr_rep = _dot(cre_ref[0], rep_p, precision=HI)
    ci_rep = _dot(cim_ref[0], rep_p, precision=HI)
    q0r = cr_rep * pr_rep - ci_rep * pi_rep
    q0i = cr_rep * pi_rep + ci_rep * pr_rep
    erc = jnp.exp(xrc)
    lbrc = erc * jnp.cos(xic)
    lbic = erc * jnp.sin(xic)
    ptr_ref[0] = (q0r * lbrc - q0i * lbic).astype(ptr_ref.dtype)
    pti_ref[0] = (-(q0r * lbic + q0i * lbrc)).astype(pti_ref.dtype)
    kern = _dot(bbr, q0r, precision=HI) - _dot(bbi, q0i, precision=HI)
    lane_w = lax.broadcasted_iota(jnp.int32, (S5_P, S5_W), 1)
    for s in range(SUBCHUNK):
        blk = kern if s == 0 else jnp.where(lane_w >= S5_P * s, pltpu.roll(kern, S5_P * s, axis=1), 0.0)
        strip_ref[0, S5_P * s:S5_P * (s + 1), :] = blk.astype(strip_ref.dtype)


def _s5_gen(lam_re, lam_im, log_dt, b_re_t, b_im_t, c_re_t, c_im_t):
    g = lam_re.shape[0]
    row3 = lambda a: a.reshape(g, 1, -1)
    col3 = lambda a: a.reshape(g, -1, 1)
    blk = lambda s: pl.BlockSpec((1,) + s, lambda i: (i, 0, 0))
    return pl.pallas_call(
        _s5_gen_kernel,
        grid=(g,),
        in_specs=[blk((1, S5_N)), blk((1, S5_N)), blk((S5_N, 1)), blk((S5_N, 1)), blk((1, 1)),
                  blk((S5_P, S5_N)), blk((S5_P, S5_N)), blk((S5_N, S5_P)), blk((S5_N, S5_P))],
        out_specs=[blk((S5_TILE, S5_W)), blk((S5_W, S5_N)), blk((S5_W, S5_N)),
                   blk((S5_N, S5_W)), blk((S5_N, S5_W)), blk((1, S5_N)), blk((1, S5_N))],
        out_shape=[
            jax.ShapeDtypeStruct((g, S5_TILE, S5_W), BF16),
            jax.ShapeDtypeStruct((g, S5_W, S5_N), BF16),
            jax.ShapeDtypeStruct((g, S5_W, S5_N), BF16),
            jax.ShapeDtypeStruct((g, S5_N, S5_W), BF16),
            jax.ShapeDtypeStruct((g, S5_N, S5_W), BF16),
            jax.ShapeDtypeStruct((g, 1, S5_N), F32),
            jax.ShapeDtypeStruct((g, 1, S5_N), F32),
        ],
        compiler_params=_cparams("parallel"),
        name="s5_gen",
    )(row3(lam_re), row3(lam_im), col3(lam_re), col3(lam_im), log_dt.reshape(g, 1, 1),
      b_re_t, b_im_t, c_re_t, c_im_t)


def _s5_apply_kernel(u_ref, strip_ref, wzr_ref, wzi_ref, ptr_ref, pti_ref, ar_ref, ai_ref,
                     y_ref, zr_scr, zi_scr, xr_scr, xi_scr, *, bsz, nchunks):
    u = u_ref[0]
    zr_scr[...] = _dot(u, wzr_ref[0])
    zi_scr[...] = _dot(u, wzi_ref[0])
    a_r = ar_ref[0]
    a_i = ai_ref[0]

    def step(c, carry):
        x_r, x_i = carry
        rows = pl.ds(pl.multiple_of(c * bsz, bsz), bsz)
        xr_scr[rows, :] = x_r
        xi_scr[rows, :] = x_i
        n_r = a_r * x_r - a_i * x_i + zr_scr[rows, :]
        n_i = a_r * x_i + a_i * x_r + zi_scr[rows, :]
        return n_r, n_i

    zero = jnp.zeros((bsz, S5_N), F32)
    lax.fori_loop(0, nchunks, step, (zero, zero))
    xr = xr_scr[...].astype(BF16)
    xi = xi_scr[...].astype(BF16)
    nt = S5_W // S5_TILE
    for j in range(nt):
        cols = slice(S5_TILE * j, S5_TILE * (j + 1))
        acc = _dot(xr, ptr_ref[0, :, cols]) + _dot(xi, pti_ref[0, :, cols])
        for i in range(j + 1):
            acc = acc + _dot(u[:, S5_TILE * i:S5_TILE * (i + 1)],
                             strip_ref[0, :, S5_TILE * (j - i):S5_TILE * (j - i + 1)])
        y_ref[0, :, cols] = acc.astype(y_ref.dtype)


def _s5_apply(ut, strip, wzr, wzi, ptr, pti, ar, ai, *, bsz, nchunks):
    g, rows, _ = ut.shape
    blk = lambda s: pl.BlockSpec((1,) + s, lambda i: (i, 0, 0))
    return pl.pallas_call(
        functools.partial(_s5_apply_kernel, bsz=bsz, nchunks=nchunks),
        grid=(g,),
        in_specs=[blk((rows, S5_W)), blk((S5_TILE, S5_W)), blk((S5_W, S5_N)), blk((S5_W, S5_N)),
                  blk((S5_N, S5_W)), blk((S5_N, S5_W)), blk((1, S5_N)), blk((1, S5_N))],
        out_specs=blk((rows, S5_W)),
        out_shape=jax.ShapeDtypeStruct((g, rows, S5_W), BF16),
        scratch_shapes=[pltpu.VMEM((rows, S5_N), F32)] * 4,
        compiler_params=_cparams("parallel"),
        name="s5_apply",
    )(ut, strip, wzr, wzi, ptr, pti, ar, ai)


def _s5_glu_kernel(h_ref, y_ref, g_ref, d_ref, w_ref, b_ref, o_ref, yv_ref, yb_ref, *, tn):
    j = pl.program_id(1)

    @pl.when(j == 0)
    def _():
        g = g_ref[...]
        dv = d_ref[...]

        def body(b, carry):
            rs = pl.ds(pl.multiple_of(b * GLU_ROWS, GLU_ROWS), GLU_ROWS)
            y = y_ref[rs, :].astype(F32) + dv * _rms(h_ref[rs, :], g)
            y = 0.5 * y * (1.0 + jnp.tanh(math.sqrt(2.0 / math.pi) * (y + 0.044715 * (y * y * y))))
            yv_ref[rs, :] = y
            yb_ref[rs, :] = y.astype(BF16)
            return carry

        lax.fori_loop(0, h_ref.shape[0] // GLU_ROWS, body, 0)

    cols = pl.ds(pl.multiple_of(j * tn, tn), tn)
    z = _dot(yb_ref[...], w_ref[...]) + b_ref[...]
    o_ref[...] = h_ref[:, cols] + yv_ref[:, cols] * _sigmoid(z)


def _s5_glu(h, y, g, dvec, w_glu, o, b_glu, *, tm, tn):
    n, d = h.shape
    return pl.pallas_call(
        functools.partial(_s5_glu_kernel, tn=tn),
        grid=(n // tm, d // tn),
        in_specs=[
            pl.BlockSpec((tm, d), lambda i, j: (i, 0)),
            pl.BlockSpec((tm, d), lambda i, j: (i, 0)),
            pl.BlockSpec((1, d), lambda i, j: (0, 0)),
            pl.BlockSpec((1, d), lambda i, j: (0, 0)),
            pl.BlockSpec((None, d, tn), lambda i, j: (o, 0, j)),
            pl.BlockSpec((1, tn), lambda i, j: (0, j)),
        ],
        out_specs=pl.BlockSpec((tm, tn), lambda i, j: (i, j)),
        out_shape=jax.ShapeDtypeStruct((n, d), F32),
        scratch_shapes=[pltpu.VMEM((tm, d), F32), pltpu.VMEM((tm, d), BF16)],
        compiler_params=_cparams("parallel", "arbitrary"),
        name="s5_glu",
    )(h, y, g, dvec, w_glu, b_glu)


def _even_layer(h, g_mix, w_main, w_gate, e, m_conv_w, m_conv_b, m_b_igate, m_b_fgate, m_head_norm,
                g_w_gate, g_b_gate, g_head_norm, w_out, *, bsz, seq, tm, tb):
    n, d = h.shape
    pm, gates = _inproj(h, g_mix.reshape(1, d), w_main, w_gate, e, tm=tm, tn=1024)
    gates_t = gates[:, 0:16].reshape(bsz, seq // CHUNK, CHUNK, 16).transpose(0, 1, 3, 2)
    bias16 = jnp.concatenate([m_b_igate, m_b_fgate]).astype(F32)
    brow = jnp.zeros((1, GATE_LANES), F32).at[0, 0:16].set(bias16)
    hm = _mlstm(pm, gates, gates_t, m_conv_w, m_conv_b.reshape(1, -1), brow, bias16.reshape(16, 1),
                m_head_norm.reshape(1, -1), bsz=bsz, seq=seq, tb=tb)
    wg_pad = jnp.zeros((GATE_LANES, G_HEADS * G_DK), BF16).at[16:16 + G_RANK].set(g_w_gate.astype(BF16))
    hg = _gla(pm, gates, wg_pad, g_b_gate.reshape(1, -1), g_head_norm.reshape(1, -1),
              bsz=bsz, seq=seq, tb=tb)
    return _outproj(hm, hg, w_out, e, h, tm=tm, tn=1024)


def _odd_layer(h, g_mix, lam_re, lam_im, log_dt, b_re, b_im, c_re, c_im, dvec, w_glu, o, b_glu,
               *, bsz, seq, tm):
    n, d = h.shape
    groups = d // S5_P
    nchunks = seq // CHUNK
    ops = _s5_gen(lam_re, lam_im, log_dt, jnp.swapaxes(b_re, 1, 2), jnp.swapaxes(b_im, 1, 2),
                  jnp.swapaxes(c_re, 1, 2), jnp.swapaxes(c_im, 1, 2))
    u = _rmsnorm(h, g_mix.reshape(1, d), BF16, tm=tm)
    ut = u.reshape(bsz, nchunks, CHUNK, groups, S5_P).transpose(3, 1, 0, 2, 4)
    ut = ut.reshape(groups, nchunks * bsz, S5_W)
    yt = _s5_apply(ut, *ops, bsz=bsz, nchunks=nchunks)
    y = yt.reshape(groups, nchunks, bsz, CHUNK, S5_P).transpose(2, 1, 3, 0, 4).reshape(n, d)
    return _s5_glu(h, y, g_mix.reshape(1, d), dvec.reshape(1, d), w_glu, o, b_glu.reshape(1, d),
                   tm=tm, tn=1024)


def _regroup_kernel(w_ref, o_ref, *, segments):
    for src, dst, n in segments:
        if src is None:
            o_ref[:, dst:dst + n] = jnp.zeros((o_ref.shape[0], n), o_ref.dtype)
        else:
            o_ref[:, dst:dst + n] = w_ref[:, src:src + n].astype(o_ref.dtype)


def _regroup_cols(w, segments, width, dtype, *, rows):
    nl, r, c = w.shape
    return pl.pallas_call(
        functools.partial(_regroup_kernel, segments=segments),
        grid=(nl, r // rows),
        in_specs=[pl.BlockSpec((None, rows, c), lambda l, i: (l, i, 0))],
        out_specs=pl.BlockSpec((None, rows, width), lambda l, i: (l, i, 0)),
        out_shape=jax.ShapeDtypeStruct((nl, r, width), dtype),
        compiler_params=_cparams("parallel", "parallel"),
        name="regroup_cols",
    )(w)


def _prep_ffn(ffn_w_up, ffn_conv_w, ffn_conv_b, ffn_w_down, tk):
    dff = ffn_w_down.shape[1]
    pad = -dff % tk
    dffp = dff + pad

    def padded(a, dtype):
        lead = a.shape[:-1]
        gv = jnp.pad(a.reshape(lead + (2, dff)), [(0, 0)] * len(lead) + [(0, 0), (0, pad)])
        return gv.astype(dtype).reshape(lead + (2 * dffp,))

    segs = ((0, 0, dff), (dff, dffp, dff)) + (((None, dff, pad), (None, dffp + dff, pad)) if pad else ())
    wup = _regroup_cols(ffn_w_up, segs, 2 * dffp, BF16, rows=128)
    cw = padded(ffn_conv_w, F32)
    cb = padded(ffn_conv_b, F32)[:, None, :]
    wdn = jnp.pad(ffn_w_down, [(0, 0), (0, pad), (0, 0)]).astype(BF16)
    return wup, cw, cb, wdn


def kernel(x, norm_mix, norm_ffn, ffn_w_up, ffn_conv_w, ffn_conv_b, ffn_w_down, norm_final,
           w_in, m_conv_w, m_conv_b, m_b_igate, m_b_fgate, m_head_norm,
           g_w_gate, g_b_gate, g_head_norm, w_out,
           s5_lambda_re, s5_lambda_im, s5_log_dt, s5_b_re, s5_b_im, s5_c_re, s5_c_im,
           s5_d, s5_w_glu, s5_b_glu):
    return _forward(x, norm_mix, norm_ffn, ffn_w_up, ffn_conv_w, ffn_conv_b, ffn_w_down, norm_final,
                    w_in, m_conv_w, m_conv_b, m_b_igate, m_b_fgate, m_head_norm,
                    g_w_gate, g_b_gate, g_head_norm, w_out,
                    s5_lambda_re, s5_lambda_im, s5_log_dt, s5_b_re, s5_b_im, s5_c_re, s5_c_im,
                    s5_d, s5_w_glu, s5_b_glu, tm=512, tb=256)


def _forward(x, norm_mix, norm_ffn, ffn_w_up, ffn_conv_w, ffn_conv_b, ffn_w_down, norm_final,
             w_in, m_conv_w, m_conv_b, m_b_igate, m_b_fgate, m_head_norm,
             g_w_gate, g_b_gate, g_head_norm, w_out,
             s5_lambda_re, s5_lambda_im, s5_log_dt, s5_b_re, s5_b_im, s5_c_re, s5_c_im,
             s5_d, s5_w_glu, s5_b_glu, *, tm, tb):
    bsz, seq, d = x.shape
    depth = norm_mix.shape[0]
    n = bsz * seq
    tmd = min(2 * tm, n)
    h = x.reshape(n, d)

    wup, cw, cb, wdn = _prep_ffn(ffn_w_up, ffn_conv_w, ffn_conv_b, ffn_w_down, 512)
    c0 = 2 * M_HEADS * M_DK + 2 * M_HEADS * M_DV
    c1 = c0 + 2 * M_HEADS
    c2 = c1 + 2 * G_HEADS * G_DK + 2 * G_HEADS * G_DV
    ng = (c1 - c0) + G_RANK
    w_main = _regroup_cols(w_in, ((0, 0, c0), (c1, c0, c2 - c1)), c0 + c2 - c1, BF16, rows=256)
    w_gate = _regroup_cols(w_in, ((c0, 0, c1 - c0), (c2, c1 - c0, G_RANK), (None, ng, GATE_LANES - ng)),
                           GATE_LANES, BF16, rows=256)
    w_out_b = w_out.astype(BF16)
    w_glu_b = s5_w_glu.astype(BF16)

    for layer in range(depth):
        if layer % 2 == 0:
            e = layer // 2
            h = _even_layer(h, norm_mix[layer], w_main, w_gate, e, m_conv_w[e], m_conv_b[e],
                            m_b_igate[e], m_b_fgate[e], m_head_norm[e], g_w_gate[e], g_b_gate[e],
                            g_head_norm[e], w_out_b, bsz=bsz, seq=seq, tm=tmd, tb=tb)
        else:
            o = layer // 2
            h = _odd_layer(h, norm_mix[layer], s5_lambda_re[o], s5_lambda_im[o], s5_log_dt[o],
                           s5_b_re[o], s5_b_im[o], s5_c_re[o], s5_c_im[o], s5_d[o], w_glu_b, o,
                           s5_b_glu[o], bsz=bsz, seq=seq, tm=tm)
        h = _ffn(h, norm_ffn[layer].reshape(1, d), wup, cw, cb, wdn, layer, seq=seq, tm=tm, tk=512)
    out = _rmsnorm(h, norm_final.reshape(1, d), F32, tm=tm)
    return out.reshape(bsz, seq, d)
```

```python
import functools
import math

import jax
import jax.numpy as jnp
from jax import lax
from jax.experimental import pallas as pl
from jax.experimental.pallas import tpu as pltpu

F32 = jnp.float32
BF16 = jnp.bfloat16
HI = lax.Precision.HIGHEST

EPS = 1e-6
CHUNK = 64
M_HEADS, M_DK, M_DV, M_CONV = 8, 64, 128, 4
G_HEADS, G_DK, G_DV, G_RANK, G_TAU = 4, 128, 256, 16, 16.0
S5_P, S5_N = 16, 64
FFN_CONV = 3
GATE_LANES = 128
SUBCHUNK = 16

VMEM_LIMIT = 56 * 1024 * 1024


def _cparams(*sem):
    return pltpu.CompilerParams(dimension_semantics=sem, vmem_limit_bytes=VMEM_LIMIT)


def _rms(x, g):
    return x * lax.rsqrt(jnp.mean(x * x, axis=-1, keepdims=True) + EPS) * g


def _sigmoid(x):
    return 1.0 / (1.0 + jnp.exp(-x))


def _log_sigmoid(x):
    return jnp.minimum(x, 0.0) - jnp.log(1.0 + jnp.exp(-jnp.abs(x)))


def _dot(a, b, **kw):
    return jnp.dot(a, b, preferred_element_type=F32, **kw)


def _dot_nt(a, b, **kw):
    return lax.dot_general(a, b, (((1,), (1,)), ((), ())), preferred_element_type=F32, **kw)


def _dot_tn(a, b, **kw):
    return lax.dot_general(a, b, (((0,), (0,)), ((), ())), preferred_element_type=F32, **kw)


def _inproj_kernel(x_ref, g_ref, w_ref, wg_ref, o_ref, og_ref, xn_ref):
    @pl.when(pl.program_id(1) == 0)
    def _():
        _rms_rows_to(xn_ref, 0, x_ref, g_ref, x_ref.shape[0])
        og_ref[...] = _dot(xn_ref[...], wg_ref[...])

    o_ref[...] = _dot(xn_ref[...], w_ref[...]).astype(o_ref.dtype)


def _inproj(h, g, w_main, w_gate, e, *, tm, tn):
    n, d = h.shape
    wn = w_main.shape[2]
    return pl.pallas_call(
        _inproj_kernel,
        grid=(n // tm, wn // tn),
        in_specs=[
            pl.BlockSpec((tm, d), lambda i, j: (i, 0)),
            pl.BlockSpec((1, d), lambda i, j: (0, 0)),
            pl.BlockSpec((None, d, tn), lambda i, j: (e, 0, j)),
            pl.BlockSpec((None, d, GATE_LANES), lambda i, j: (e, 0, 0)),
        ],
        out_specs=[
            pl.BlockSpec((tm, tn), lambda i, j: (i, j)),
            pl.BlockSpec((tm, GATE_LANES), lambda i, j: (i, 0)),
        ],
        out_shape=[
            jax.ShapeDtypeStruct((n, wn), BF16),
            jax.ShapeDtypeStruct((n, GATE_LANES), F32),
        ],
        scratch_shapes=[pltpu.VMEM((tm, d), BF16)],
        compiler_params=_cparams("parallel", "arbitrary"),
        name="inproj",
    )(h, g, w_main, w_gate)


def _mlstm_kernel(mq_ref, mk_ref, mv_ref, mo_ref, gc_ref, gr_ref, cw_ref, cb_ref,
                  brow_ref, bcol_ref, hn_ref, o_ref, qk_scr, qkc_scr, c_scr, m_scr, *, tb):
    @pl.when(pl.program_id(1) == 0)
    def _():
        qk_scr[0:8, :] = jnp.zeros((8, 2 * M_HEADS * M_DK), F32)
        c_scr[...] = jnp.zeros(c_scr.shape, F32)
        m_scr[...] = jnp.zeros(m_scr.shape, F32)

    nqk = M_HEADS * M_DK
    qk_scr[8:8 + tb, 0:nqk] = mq_ref[...].astype(F32)
    qk_scr[8:8 + tb, nqk:2 * nqk] = mk_ref[...].astype(F32)
    conv = cb_ref[...] + cw_ref[0:1, :] * qk_scr[5:5 + tb, :]
    for j in range(1, M_CONV):
        conv = conv + cw_ref[j:j + 1, :] * qk_scr[5 + j:5 + j + tb, :]
    tail = qk_scr[tb:tb + 8, :]
    qkc_scr[...] = conv * _sigmoid(conv)
    qk_scr[0:8, :] = tail

    row = lax.broadcasted_iota(jnp.int32, (CHUNK, CHUNK), 0)
    col = lax.broadcasted_iota(jnp.int32, (CHUNK, CHUNK), 1)
    tril = row >= col
    tri = tril.astype(F32)
    tri_t = (row <= col).astype(F32)
    lane128 = lax.broadcasted_iota(jnp.int32, (1, 2 * M_DK), 1)
    row128 = lax.broadcasted_iota(jnp.int32, (2 * M_DK, 1), 0)
    ones_v = jnp.ones((CHUNK, M_DV), BF16)
    scale = M_DK ** -0.5

    def chunk(c, carry):
        r0 = pl.multiple_of(c * CHUNK, CHUNK)
        rows = pl.ds(r0, CHUNK)
        gcol = gc_ref[rows, :] + brow_ref[...]
        bcum = _dot(tri, _log_sigmoid(gcol), precision=HI)
        grow = gr_ref[0, c] + bcol_ref[...]
        li_rows = grow[0:M_HEADS]
        b_rows = _dot(_log_sigmoid(grow[M_HEADS:2 * M_HEADS]), tri_t, precision=HI)
        for p in range(M_HEADS // 2):
            qp = qkc_scr[rows, 2 * M_DK * p:2 * M_DK * (p + 1)] * scale
            kp = qkc_scr[rows, nqk + 2 * M_DK * p:nqk + 2 * M_DK * (p + 1)]
            kpb = kp.astype(BF16)
            cp = c_scr[p]
            cpb = cp.astype(BF16)
            upd = None
            decays = []
            for e in range(2):
                h = 2 * p + e
                lmask = (lane128 // M_DK) == e
                qm = jnp.where(lmask, qp, 0.0).astype(BF16)
                b_col = bcum[:, M_HEADS + h:M_HEADS + h + 1]
                li_col = gcol[:, h:h + 1]
                b_row = b_rows[h:h + 1, :]
                li_row = li_rows[h:h + 1, :]
                b_last = b_row[:, CHUNK - 1:CHUNK]
                m_prev = m_scr[h][0:1, 0:1]
                dmat = jnp.where(tril, b_col - b_row + li_row, -jnp.inf)
                inter = b_col + m_prev
                m_out = jnp.maximum(inter, jnp.max(dmat, axis=-1, keepdims=True))
                smat = _dot_nt(qm, kpb) * jnp.exp(dmat - m_out)
                w_inter = jnp.exp(inter - m_out)
                vext = jnp.concatenate([mv_ref[rows, M_DV * h:M_DV * (h + 1)], ones_v], axis=-1)
                numext = _dot(smat.astype(BF16), vext) + w_inter * _dot(qm, cpb)
                num = numext[:, 0:M_DV]
                den = numext[:, M_DV:2 * M_DV]
                hh = num / jnp.maximum(jnp.abs(den), jnp.exp(-m_out))
                y = hh * lax.rsqrt(jnp.mean(hh * hh, axis=-1, keepdims=True) + EPS)
                y = y * hn_ref[:, M_DV * h:M_DV * (h + 1)]
                og = _sigmoid(mo_ref[rows, M_DV * h:M_DV * (h + 1)].astype(F32))
                o_ref[rows, M_DV * h:M_DV * (h + 1)] = (og * y).astype(o_ref.dtype)
                g_row = b_last - b_row + li_row
                m_new = jnp.maximum(b_last + m_prev, jnp.max(g_row, axis=-1, keepdims=True))
                wk_col = jnp.exp(b_last - b_col + li_col - m_new)
                decays.append(jnp.exp(b_last + m_prev - m_new))
                kw = jnp.where(lmask, kp * wk_col, 0.0).astype(BF16)
                u = _dot_tn(kw, vext)
                upd = u if upd is None else upd + u
                m_scr[h] = jnp.broadcast_to(m_new, m_scr.shape[1:])
            dcol = jnp.where(row128 < M_DK, decays[0], decays[1])
            c_scr[p] = dcol * cp + upd
        return carry

    lax.fori_loop(0, tb // CHUNK, chunk, 0)


def _mlstm(pm, gates, gates_t, conv_w, conv_b, brow, bcol, hnorm, *, bsz, seq, tb):
    n = pm.shape[0]
    nt = seq // tb
    nqk = M_HEADS * M_DK
    nv = M_HEADS * M_DV
    rowmap = lambda b, t: (b * nt + t, 0)
    return pl.pallas_call(
        functools.partial(_mlstm_kernel, tb=tb),
        grid=(bsz, nt),
        in_specs=[
            pl.BlockSpec((tb, nqk), lambda b, t: (b * nt + t, 0)),
            pl.BlockSpec((tb, nqk), lambda b, t: (b * nt + t, 1)),
            pl.BlockSpec((tb, nv), lambda b, t: (b * nt + t, 1)),
            pl.BlockSpec((tb, nv), lambda b, t: (b * nt + t, 2)),
            pl.BlockSpec((tb, GATE_LANES), rowmap),
            pl.BlockSpec((1, tb // CHUNK, 16, CHUNK), lambda b, t: (b, t, 0, 0)),
            pl.BlockSpec((M_CONV, 2 * nqk), lambda b, t: (0, 0)),
            pl.BlockSpec((1, 2 * nqk), lambda b, t: (0, 0)),
            pl.BlockSpec((1, GATE_LANES), lambda b, t: (0, 0)),
            pl.BlockSpec((16, 1), lambda b, t: (0, 0)),
            pl.BlockSpec((1, nv), lambda b, t: (0, 0)),
        ],
        out_specs=pl.BlockSpec((tb, nv), rowmap),
        out_shape=jax.ShapeDtypeStruct((n, nv), BF16),
        scratch_shapes=[
            pltpu.VMEM((tb + 8, 2 * nqk), F32),
            pltpu.VMEM((tb, 2 * nqk), F32),
            pltpu.VMEM((M_HEADS // 2, 2 * M_DK, 2 * M_DV), F32),
            pltpu.VMEM((M_HEADS, 8, 128), F32),
        ],
        compiler_params=_cparams("parallel", "arbitrary"),
        name="mlstm",
    )(pm, pm, pm, pm, gates, gates_t, conv_w, conv_b, brow, bcol, hnorm)


def _gla_kernel(gq_ref, gk_ref, gv_ref, gg_ref, gc_ref, wg_ref, bg_ref, hn_ref, o_ref, s_scr, *, tb):
    @pl.when(pl.program_id(1) == 0)
    def _():
        s_scr[...] = jnp.zeros(s_scr.shape, F32)

    row = lax.broadcasted_iota(jnp.int32, (CHUNK, CHUNK), 0)
    col = lax.broadcasted_iota(jnp.int32, (CHUNK, CHUNK), 1)
    tril = row >= col
    tri = tril.astype(F32)
    rowk = lax.broadcasted_iota(jnp.int32, (CHUNK, 1), 0)
    nsub = CHUNK // SUBCHUNK
    scale = G_DK ** -0.5

    def chunk(c, carry):
        r0 = pl.multiple_of(c * CHUNK, CHUNK)
        rows = pl.ds(r0, CHUNK)
        pre = _dot(gc_ref[rows, :].astype(BF16), wg_ref[...]) + bg_ref[...]
        la = _log_sigmoid(pre) * (1.0 / G_TAU)
        bc_all = _dot(tri, la, precision=HI)
        for h in range(G_HEADS):
            ks = slice(G_DK * h, G_DK * (h + 1))
            vs = slice(G_DV * h, G_DV * (h + 1))
            bc = bc_all[:, ks]
            q = gq_ref[rows, ks].astype(F32) * scale
            k = gk_ref[rows, ks].astype(F32)
            v = gv_ref[rows, vs]
            st = s_scr[h]
            o = _dot_nt((q * jnp.exp(bc)).astype(BF16), st.astype(BF16))
            cblk = jnp.concatenate(
                [jnp.broadcast_to(bc[SUBCHUNK * i:SUBCHUNK * i + 1, :], (SUBCHUNK, G_DK))
                 for i in range(nsub)], axis=0)
            qt = (q * jnp.exp(bc - cblk)).astype(BF16)
            blocks = []
            for i in range(nsub):
                ci = bc[SUBCHUNK * i:SUBCHUNK * i + 1, :]
                kt = jnp.where(rowk < SUBCHUNK * (i + 1), k * jnp.exp(ci - bc), 0.0).astype(BF16)
                blocks.append(_dot_nt(qt[SUBCHUNK * i:SUBCHUNK * (i + 1), :], kt))
            a = jnp.where(tril, jnp.concatenate(blocks, axis=0), 0.0)
            o = o + _dot(a.astype(BF16), v)
            y = o * lax.rsqrt(jnp.mean(o * o, axis=-1, keepdims=True) + EPS) * hn_ref[:, vs]
            gg = gg_ref[rows, vs].astype(F32)
            o_ref[rows, vs] = (gg * _sigmoid(gg) * y).astype(o_ref.dtype)
            last = bc[CHUNK - 1:CHUNK, :]
            kd = (k * jnp.exp(last - bc)).astype(BF16)
            s_scr[h] = st * jnp.exp(last) + _dot_tn(v, kd)
        return carry

    lax.fori_loop(0, tb // CHUNK, chunk, 0)


def _gla(pm, gates, wg_pad, bg, hnorm, *, bsz, seq, tb):
    n = pm.shape[0]
    nt = seq // tb
    nqk = G_HEADS * G_DK
    nv = G_HEADS * G_DV
    rowmap = lambda b, t: (b * nt + t, 0)
    return pl.pallas_call(
        functools.partial(_gla_kernel, tb=tb),
        grid=(bsz, nt),
        in_specs=[
            pl.BlockSpec((tb, nqk), lambda b, t: (b * nt + t, 6)),
            pl.BlockSpec((tb, nqk), lambda b, t: (b * nt + t, 7)),
            pl.BlockSpec((tb, nv), lambda b, t: (b * nt + t, 4)),
            pl.BlockSpec((tb, nv), lambda b, t: (b * nt + t, 5)),
            pl.BlockSpec((tb, GATE_LANES), rowmap),
            pl.BlockSpec((GATE_LANES, nqk), lambda b, t: (0, 0)),
            pl.BlockSpec((1, nqk), lambda b, t: (0, 0)),
            pl.BlockSpec((1, nv), lambda b, t: (0, 0)),
        ],
        out_specs=pl.BlockSpec((tb, nv), rowmap),
        out_shape=jax.ShapeDtypeStruct((n, nv), BF16),
        scratch_shapes=[pltpu.VMEM((G_HEADS, G_DV, G_DK), F32)],
        compiler_params=_cparams("parallel", "arbitrary"),
        name="gla",
    )(pm, pm, pm, pm, gates, wg_pad, bg, hnorm)


def _outproj_kernel(hm_ref, hg_ref, w1_ref, w2_ref, h_ref, o_ref):
    o_ref[...] = h_ref[...] + _dot(hm_ref[...], w1_ref[...]) + _dot(hg_ref[...], w2_ref[...])


def _outproj(hm, hg, w_out, e, h, *, tm, tn):
    n, d = h.shape
    kh = hm.shape[1]
    return pl.pallas_call(
        _outproj_kernel,
        grid=(n // tm, d // tn),
        in_specs=[
            pl.BlockSpec((tm, kh), lambda i, j: (i, 0)),
            pl.BlockSpec((tm, kh), lambda i, j: (i, 0)),
            pl.BlockSpec((None, kh, tn), lambda i, j: (e, 0, j)),
            pl.BlockSpec((None, kh, tn), lambda i, j: (e, 1, j)),
            pl.BlockSpec((tm, tn), lambda i, j: (i, j)),
        ],
        out_specs=pl.BlockSpec((tm, tn), lambda i, j: (i, j)),
        out_shape=jax.ShapeDtypeStruct((n, d), F32),
        compiler_params=_cparams("parallel", "arbitrary"),
        name="outproj",
    )(hm, hg, w_out, w_out, h)


FFN_HALO = 16
FFN_COLS = 256
NORM_ROWS = 128
GLU_ROWS = 64


def _rms_rows_to(dst_ref, dst_off, src_ref, g_ref, rows):
    g = g_ref[...]
    for r0 in range(0, rows, NORM_ROWS):
        nr = min(NORM_ROWS, rows - r0)
        dst_ref[dst_off + r0:dst_off + r0 + nr, :] = _rms(src_ref[r0:r0 + nr, :], g).astype(dst_ref.dtype)


def _ffn_kernel(h_ref, halo_ref, g_ref, wup_ref, cwg_ref, cwv_ref, cbg_ref, cbv_ref, wdn_ref,
                o_ref, xn_ref, ag_ref, av_ref, act_ref, *, tm, tk, nk, tiles_per_seq):
    i = pl.program_id(0)
    k = pl.program_id(1)
    ncol = tk // FFN_COLS

    @pl.when(k == 0)
    def _():
        _rms_rows_to(xn_ref, 0, halo_ref, g_ref, FFN_HALO)
        _rms_rows_to(xn_ref, FFN_HALO, h_ref, g_ref, tm)
        o_ref[...] = h_ref[...]

    keep = ((i % tiles_per_seq) != 0).astype(F32)

    def up(c):
        cs = slice(FFN_COLS * c, FFN_COLS * (c + 1))
        ag_ref[c] = _dot(xn_ref[...], wup_ref[:, cs])
        av_ref[c] = _dot(xn_ref[...], wup_ref[:, tk + FFN_COLS * c:tk + FFN_COLS * (c + 1)])

    def conv(a_ref, c, cw_ref, cb_ref):
        cs = slice(FFN_COLS * c, FFN_COLS * (c + 1))
        a_ref[c, 0:FFN_HALO, :] = a_ref[c, 0:FFN_HALO, :] * keep
        return (cw_ref[2:3, cs] * a_ref[c, FFN_HALO:FFN_HALO + tm, :]
                + cw_ref[1:2, cs] * a_ref[c, FFN_HALO - 1:FFN_HALO - 1 + tm, :]
                + cw_ref[0:1, cs] * a_ref[c, FFN_HALO - 2:FFN_HALO - 2 + tm, :] + cb_ref[:, cs])

    def gating(c, slot):
        gate = conv(ag_ref, c, cwg_ref, cbg_ref)
        val = conv(av_ref, c, cwv_ref, cbv_ref)
        act_ref[slot, :, FFN_COLS * c:FFN_COLS * (c + 1)] = (gate * _sigmoid(gate) * val).astype(BF16)

    def step(slot_w, slot_r):
        if slot_w is not None:
            for c in range(ncol):
                up(c)
        if slot_r is not None:
            o_ref[...] += _dot(act_ref[slot_r], wdn_ref[...])
        if slot_w is not None:
            for c in range(ncol):
                gating(c, slot_w)

    @pl.when(k == 0)
    def _():
        step(0, None)

    for par in range(2):
        @pl.when((k > 0) & (k < nk) & (k % 2 == par))
        def _():
            step(par, 1 - par)

    @pl.when(k == nk)
    def _():
        step(None, (nk - 1) % 2)


def _ffn(h, g, wup, cw, cb, wdn, layer, *, seq, tm, tk):
    n, d = h.shape
    nk = wdn.shape[1] // tk
    hb = tm // FFN_HALO
    upk = lambda k: jnp.minimum(k, nk - 1)
    dnk = lambda k: jnp.maximum(k - 1, 0)
    return pl.pallas_call(
        functools.partial(_ffn_kernel, tm=tm, tk=tk, nk=nk, tiles_per_seq=seq // tm),
        grid=(n // tm, nk + 1),
        in_specs=[
            pl.BlockSpec((tm, d), lambda i, k: (i, 0)),
            pl.BlockSpec((FFN_HALO, d), lambda i, k: (jnp.maximum(i * hb - 1, 0), 0)),
            pl.BlockSpec((1, d), lambda i, k: (0, 0)),
            pl.BlockSpec((None, None, d, 2 * tk), lambda i, k: (layer, upk(k), 0, 0)),
            pl.BlockSpec((None, FFN_CONV, tk), lambda i, k: (layer, 0, upk(k))),
            pl.BlockSpec((None, FFN_CONV, tk), lambda i, k: (layer, 0, nk + upk(k))),
            pl.BlockSpec((None, 1, tk), lambda i, k: (layer, 0, upk(k))),
            pl.BlockSpec((None, 1, tk), lambda i, k: (layer, 0, nk + upk(k))),
            pl.BlockSpec((None, tk, d), lambda i, k: (layer, dnk(k), 0)),
        ],
        out_specs=pl.BlockSpec((tm, d), lambda i, k: (i, 0)),
        out_shape=jax.ShapeDtypeStruct((n, d), F32),
        scratch_shapes=[
            pltpu.VMEM((tm + FFN_HALO, d), BF16),
            pltpu.VMEM((tk // FFN_COLS, tm + FFN_HALO, FFN_COLS), F32),
            pltpu.VMEM((tk // FFN_COLS, tm + FFN_HALO, FFN_COLS), F32),
            pltpu.VMEM((2, tm, tk), BF16),
        ],
        compiler_params=_cparams("parallel", "arbitrary"),
        name="convffn",
    )(h, h, g, wup, cw, cw, cb, cb, wdn)


def _rmsnorm_kernel(x_ref, g_ref, o_ref):
    o_ref[...] = _rms(x_ref[...], g_ref[...]).astype(o_ref.dtype)


def _rmsnorm(h, g, dtype, *, tm):
    n, d = h.shape
    return pl.pallas_call(
        _rmsnorm_kernel,
        grid=(n // tm,),
        in_specs=[pl.BlockSpec((tm, d), lambda i: (i, 0)), pl.BlockSpec((1, d), lambda i: (0, 0))],
        out_specs=pl.BlockSpec((tm, d), lambda i: (i, 0)),
        out_shape=jax.ShapeDtypeStruct((n, d), dtype),
        compiler_params=_cparams("parallel"),
        name="rmsnorm",
    )(h, g)


S5_TILE = SUBCHUNK * S5_P
S5_W = CHUNK * S5_P


def _s5_gen_kernel(lr_ref, li_ref, lrc_ref, lic_ref, dt_ref, bre_ref, bim_ref, cre_ref, cim_ref,
                   strip_ref, wzr_ref, wzi_ref, ptr_ref, pti_ref, ar_ref, ai_ref):
    dt = jnp.exp(dt_ref[0])
    lam_r = lr_ref[0]
    lam_i = li_ref[0]
    xr = lam_r * dt
    xi = lam_i * dt
    er = jnp.exp(xr)
    lbr = er * jnp.cos(xi)
    lbi = er * jnp.sin(xi)
    den = lam_r * lam_r + lam_i * lam_i
    cfr = ((lbr - 1.0) * lam_r + lbi * lam_i) / den
    cfi = (lbi * lam_r - (lbr - 1.0) * lam_i) / den
    bbr = cfr * bre_ref[0] - cfi * bim_ref[0]
    bbi = cfr * bim_ref[0] + cfi * bre_ref[0]
    kk = lax.broadcasted_iota(jnp.int32, (CHUNK, S5_N), 0).astype(F32)
    pe = jnp.exp(kk * xr)
    pwr = pe * jnp.cos(kk * xi)
    pwi = pe * jnp.sin(kk * xi)
    for s in range(CHUNK):
        pr = pwr[CHUNK - 1 - s:CHUNK - s, :]
        pi = pwi[CHUNK - 1 - s:CHUNK - s, :]
        wzr_ref[0, S5_P * s:S5_P * (s + 1), :] = (bbr * pr - bbi * pi).astype(wzr_ref.dtype)
        wzi_ref[0, S5_P * s:S5_P * (s + 1), :] = (bbr * pi + bbi * pr).astype(wzi_ref.dtype)
    e64 = jnp.exp(CHUNK * xr)
    ar_ref[0] = e64 * jnp.cos(CHUNK * xi)
    ai_ref[0] = e64 * jnp.sin(CHUNK * xi)
    xrc = lrc_ref[0] * dt
    xic = lic_ref[0] * dt
    tt = lax.broadcasted_iota(jnp.int32, (S5_N, CHUNK), 1).astype(F32)
    pte = jnp.exp(tt * xrc)
    ptr = pte * jnp.cos(tt * xic)
    pti = pte * jnp.sin(tt * xic)
    lane = lax.broadcasted_iota(jnp.int32, (CHUNK, S5_W), 1)
    rep_t = ((lane // S5_P) == lax.broadcasted_iota(jnp.int32, (CHUNK, S5_W), 0)).astype(F32)
    lane_p = lax.broadcasted_iota(jnp.int32, (S5_P, S5_W), 1)
    rep_p = ((lane_p % S5_P) == lax.broadcasted_iota(jnp.int32, (S5_P, S5_W), 0)).astype(F32)
    pr_rep = _dot(ptr, rep_t, precision=HI)
    pi_rep = _dot(pti, rep_t, precision=HI)
    cr_rep = _dot(cre_ref[0], rep_p, precision=HI)
    ci_rep = _dot(cim_ref[0], rep_p, precision=HI)
    q0r = cr_rep * pr_rep - ci_rep * pi_rep
    q0i = cr_rep * pi_rep + ci_rep * pr_rep
    erc = jnp.exp(xrc)
    lbrc = erc * jnp.cos(xic)
    lbic = erc * jnp.sin(xic)
    ptr_ref[0] = (q0r * lbrc - q0i * lbic).astype(ptr_ref.dtype)
    pti_ref[0] = (-(q0r * lbic + q0i * lbrc)).astype(pti_ref.dtype)
    kern = _dot(bbr, q0r, precision=HI) - _dot(bbi, q0i, precision=HI)
    lane_w = lax.broadcasted_iota(jnp.int32, (S5_P, S5_W), 1)
    for s in range(SUBCHUNK):
        blk = kern if s == 0 else jnp.where(lane_w >= S5_P * s, pltpu.roll(kern, S5_P * s, axis=1), 0.0)
        strip_ref[0, S5_P * s:S5_P * (s + 1), :] = blk.astype(strip_ref.dtype)


def _s5_gen(lam_re, lam_im, log_dt, b_re_t, b_im_t, c_re_t, c_im_t):
    g = lam_re.shape[0]
    row3 = lambda a: a.reshape(g, 1, -1)
    col3 = lambda a: a.reshape(g, -1, 1)
    blk = lambda s: pl.BlockSpec((1,) + s, lambda i: (i, 0, 0))
    return pl.pallas_call(
        _s5_gen_kernel,
        grid=(g,),
        in_specs=[blk((1, S5_N)), blk((1, S5_N)), blk((S5_N, 1)), blk((S5_N, 1)), blk((1, 1)),
                  blk((S5_P, S5_N)), blk((S5_P, S5_N)), blk((S5_N, S5_P)), blk((S5_N, S5_P))],
        out_specs=[blk((S5_TILE, S5_W)), blk((S5_W, S5_N)), blk((S5_W, S5_N)),
                   blk((S5_N, S5_W)), blk((S5_N, S5_W)), blk((1, S5_N)), blk((1, S5_N))],
        out_shape=[
            jax.ShapeDtypeStruct((g, S5_TILE, S5_W), BF16),
            jax.ShapeDtypeStruct((g, S5_W, S5_N), BF16),
            jax.ShapeDtypeStruct((g, S5_W, S5_N), BF16),
            jax.ShapeDtypeStruct((g, S5_N, S5_W), BF16),
            jax.ShapeDtypeStruct((g, S5_N, S5_W), BF16),
            jax.ShapeDtypeStruct((g, 1, S5_N), F32),
            jax.ShapeDtypeStruct((g, 1, S5_N), F32),
        ],
        compiler_params=_cparams("parallel"),
        name="s5_gen",
    )(row3(lam_re), row3(lam_im), col3(lam_re), col3(lam_im), log_dt.reshape(g, 1, 1),
      b_re_t, b_im_t, c_re_t, c_im_t)


def _s5_apply_kernel(u_ref, strip_ref, wzr_ref, wzi_ref, ptr_ref, pti_ref, ar_ref, ai_ref,
                     y_ref, zr_scr, zi_scr, xr_scr, xi_scr, *, bsz, nchunks):
    u = u_ref[0]
    zr_scr[...] = _dot(u, wzr_ref[0])
    zi_scr[...] = _dot(u, wzi_ref[0])
    a_r = ar_ref[0]
    a_i = ai_ref[0]

    def step(c, carry):
        x_r, x_i = carry
        rows = pl.ds(pl.multiple_of(c * bsz, bsz), bsz)
        xr_scr[rows, :] = x_r
        xi_scr[rows, :] = x_i
        n_r = a_r * x_r - a_i * x_i + zr_scr[rows, :]
        n_i = a_r * x_i + a_i * x_r + zi_scr[rows, :]
        return n_r, n_i

    zero = jnp.zeros((bsz, S5_N), F32)
    lax.fori_loop(0, nchunks, step, (zero, zero))
    xr = xr_scr[...].astype(BF16)
    xi = xi_scr[...].astype(BF16)
    nt = S5_W // S5_TILE
    for j in range(nt):
        cols = slice(S5_TILE * j, S5_TILE * (j + 1))
        acc = _dot(xr, ptr_ref[0, :, cols]) + _dot(xi, pti_ref[0, :, cols])
        for i in range(j + 1):
            acc = acc + _dot(u[:, S5_TILE * i:S5_TILE * (i + 1)],
                             strip_ref[0, :, S5_TILE * (j - i):S5_TILE * (j - i + 1)])
        y_ref[0, :, cols] = acc.astype(y_ref.dtype)


def _s5_apply(ut, strip, wzr, wzi, ptr, pti, ar, ai, *, bsz, nchunks):
    g, rows, _ = ut.shape
    blk = lambda s: pl.BlockSpec((1,) + s, lambda i: (i, 0, 0))
    return pl.pallas_call(
        functools.partial(_s5_apply_kernel, bsz=bsz, nchunks=nchunks),
        grid=(g,),
        in_specs=[blk((rows, S5_W)), blk((S5_TILE, S5_W)), blk((S5_W, S5_N)), blk((S5_W, S5_N)),
                  blk((S5_N, S5_W)), blk((S5_N, S5_W)), blk((1, S5_N)), blk((1, S5_N))],
        out_specs=blk((rows, S5_W)),
        out_shape=jax.ShapeDtypeStruct((g, rows, S5_W), BF16),
        scratch_shapes=[pltpu.VMEM((rows, S5_N), F32)] * 4,
        compiler_params=_cparams("parallel"),
        name="s5_apply",
    )(ut, strip, wzr, wzi, ptr, pti, ar, ai)


def _s5_glu_kernel(h_ref, y_ref, g_ref, d_ref, w_ref, b_ref, o_ref, yv_ref, yb_ref, *, tn):
    j = pl.program_id(1)

    @pl.when(j == 0)
    def _():
        g = g_ref[...]
        dv = d_ref[...]

        def body(b, carry):
            rs = pl.ds(pl.multiple_of(b * GLU_ROWS, GLU_ROWS), GLU_ROWS)
            y = y_ref[rs, :].astype(F32) + dv * _rms(h_ref[rs, :], g)
            y = 0.5 * y * (1.0 + jnp.tanh(math.sqrt(2.0 / math.pi) * (y + 0.044715 * (y * y * y))))
            yv_ref[rs, :] = y
            yb_ref[rs, :] = y.astype(BF16)
            return carry

        lax.fori_loop(0, h_ref.shape[0] // GLU_ROWS, body, 0)

    cols = pl.ds(pl.multiple_of(j * tn, tn), tn)
    z = _dot(yb_ref[...], w_ref[...]) + b_ref[...]
    o_ref[...] = h_ref[:, cols] + yv_ref[:, cols] * _sigmoid(z)


def _s5_glu(h, y, g, dvec, w_glu, o, b_glu, *, tm, tn):
    n, d = h.shape
    return pl.pallas_call(
        functools.partial(_s5_glu_kernel, tn=tn),
        grid=(n // tm, d // tn),
        in_specs=[
            pl.BlockSpec((tm, d), lambda i, j: (i, 0)),
            pl.BlockSpec((tm, d), lambda i, j: (i, 0)),
            pl.BlockSpec((1, d), lambda i, j: (0, 0)),
            pl.BlockSpec((1, d), lambda i, j: (0, 0)),
            pl.BlockSpec((None, d, tn), lambda i, j: (o, 0, j)),
            pl.BlockSpec((1, tn), lambda i, j: (0, j)),
        ],
        out_specs=pl.BlockSpec((tm, tn), lambda i, j: (i, j)),
        out_shape=jax.ShapeDtypeStruct((n, d), F32),
        scratch_shapes=[pltpu.VMEM((tm, d), F32), pltpu.VMEM((tm, d), BF16)],
        compiler_params=_cparams("parallel", "arbitrary"),
        name="s5_glu",
    )(h, y, g, dvec, w_glu, b_glu)


def _even_layer(h, g_mix, w_main, w_gate, e, m_conv_w, m_conv_b, m_b_igate, m_b_fgate, m_head_norm,
                g_w_gate, g_b_gate, g_head_norm, w_out, *, bsz, seq, tm, tb):
    n, d = h.shape
    pm, gates = _inproj(h, g_mix.reshape(1, d), w_main, w_gate, e, tm=tm, tn=1024)
    gates_t = gates[:, 0:16].reshape(bsz, seq // CHUNK, CHUNK, 16).transpose(0, 1, 3, 2)
    bias16 = jnp.concatenate([m_b_igate, m_b_fgate]).astype(F32)
    brow = jnp.zeros((1, GATE_LANES), F32).at[0, 0:16].set(bias16)
    hm = _mlstm(pm, gates, gates_t, m_conv_w, m_conv_b.reshape(1, -1), brow, bias16.reshape(16, 1),
                m_head_norm.reshape(1, -1), bsz=bsz, seq=seq, tb=tb)
    wg_pad = jnp.zeros((GATE_LANES, G_HEADS * G_DK), BF16).at[16:16 + G_RANK].set(g_w_gate.astype(BF16))
    hg = _gla(pm, gates, wg_pad, g_b_gate.reshape(1, -1), g_head_norm.reshape(1, -1),
              bsz=bsz, seq=seq, tb=tb)
    return _outproj(hm, hg, w_out, e, h, tm=tm, tn=1024)


def _odd_layer(h, g_mix, lam_re, lam_im, log_dt, b_re, b_im, c_re, c_im, dvec, w_glu, o, b_glu,
               *, bsz, seq, tm):
    n, d = h.shape
    groups = d // S5_P
    nchunks = seq // CHUNK
    ops = _s5_gen(lam_re, lam_im, log_dt, jnp.swapaxes(b_re, 1, 2), jnp.swapaxes(b_im, 1, 2),
                  jnp.swapaxes(c_re, 1, 2), jnp.swapaxes(c_im, 1, 2))
    u = _rmsnorm(h, g_mix.reshape(1, d), BF16, tm=tm)
    ut = u.reshape(bsz, nchunks, CHUNK, groups, S5_P).transpose(3, 1, 0, 2, 4)
    ut = ut.reshape(groups, nchunks * bsz, S5_W)
    yt = _s5_apply(ut, *ops, bsz=bsz, nchunks=nchunks)
    y = yt.reshape(groups, nchunks, bsz, CHUNK, S5_P).transpose(2, 1, 3, 0, 4).reshape(n, d)
    return _s5_glu(h, y, g_mix.reshape(1, d), dvec.reshape(1, d), w_glu, o, b_glu.reshape(1, d),
                   tm=tm, tn=1024)


def _regroup_kernel(w_ref, o_ref, *, segments):
    for src, dst, n in segments:
        if src is None:
            o_ref[:, dst:dst + n] = jnp.zeros((o_ref.shape[0], n), o_ref.dtype)
        else:
            o_ref[:, dst:dst + n] = w_ref[:, src:src + n].astype(o_ref.dtype)


def _regroup_cols(w, segments, width, dtype, *, rows):
    nl, r, c = w.shape
    return pl.pallas_call(
        functools.partial(_regroup_kernel, segments=segments),
        grid=(nl, r // rows),
        in_specs=[pl.BlockSpec((None, rows, c), lambda l, i: (l, i, 0))],
        out_specs=pl.BlockSpec((None, rows, width), lambda l, i: (l, i, 0)),
        out_shape=jax.ShapeDtypeStruct((nl, r, width), dtype),
        compiler_params=_cparams("parallel", "parallel"),
        name="regroup_cols",
    )(w)


def _tile_up_kernel(w_ref, o_ref, *, dff, tk):
    for k in range(o_ref.shape[0]):
        n = min(tk, dff - k * tk)
        for half, base in ((0, 0), (1, dff)):
            o_ref[k, :, half * tk:half * tk + n] = w_ref[:, base + k * tk:base + k * tk + n].astype(o_ref.dtype)
            if n < tk:
                o_ref[k, :, half * tk + n:(half + 1) * tk] = jnp.zeros((o_ref.shape[1], tk - n), o_ref.dtype)


def _tile_up(w, tk, *, rows):
    nl, d, two_ff = w.shape
    dff = two_ff // 2
    nk = -(-dff // tk)
    return pl.pallas_call(
        functools.partial(_tile_up_kernel, dff=dff, tk=tk),
        grid=(nl, d // rows),
        in_specs=[pl.BlockSpec((None, rows, two_ff), lambda l, i: (l, i, 0))],
        out_specs=pl.BlockSpec((None, nk, rows, 2 * tk), lambda l, i: (l, 0, i, 0)),
        out_shape=jax.ShapeDtypeStruct((nl, nk, d, 2 * tk), BF16),
        compiler_params=_cparams("parallel", "parallel"),
        name="tile_up",
    )(w)


def _prep_ffn(ffn_w_up, ffn_conv_w, ffn_conv_b, ffn_w_down, tk):
    dff = ffn_w_down.shape[1]
    pad = -dff % tk
    dffp = dff + pad

    def padded(a, dtype):
        lead = a.shape[:-1]
        gv = jnp.pad(a.reshape(lead + (2, dff)), [(0, 0)] * len(lead) + [(0, 0), (0, pad)])
        return gv.astype(dtype).reshape(lead + (2 * dffp,))

    wup = _tile_up(ffn_w_up, tk, rows=128)
    cw = padded(ffn_conv_w, F32)
    cb = padded(ffn_conv_b, F32)[:, None, :]
    wdn = jnp.pad(ffn_w_down, [(0, 0), (0, pad), (0, 0)]).astype(BF16)
    return wup, cw, cb, wdn


def kernel(x, norm_mix, norm_ffn, ffn_w_up, ffn_conv_w, ffn_conv_b, ffn_w_down, norm_final,
           w_in, m_conv_w, m_conv_b, m_b_igate, m_b_fgate, m_head_norm,
           g_w_gate, g_b_gate, g_head_norm, w_out,
           s5_lambda_re, s5_lambda_im, s5_log_dt, s5_b_re, s5_b_im, s5_c_re, s5_c_im,
           s5_d, s5_w_glu, s5_b_glu):
    return _forward(x, norm_mix, norm_ffn, ffn_w_up, ffn_conv_w, ffn_conv_b, ffn_w_down, norm_final,
                    w_in, m_conv_w, m_conv_b, m_b_igate, m_b_fgate, m_head_norm,
                    g_w_gate, g_b_gate, g_head_norm, w_out,
                    s5_lambda_re, s5_lambda_im, s5_log_dt, s5_b_re, s5_b_im, s5_c_re, s5_c_im,
                    s5_d, s5_w_glu, s5_b_glu, tm=512, tb=512)


def _forward(x, norm_mix, norm_ffn, ffn_w_up, ffn_conv_w, ffn_conv_b, ffn_w_down, norm_final,
             w_in, m_conv_w, m_conv_b, m_b_igate, m_b_fgate, m_head_norm,
             g_w_gate, g_b_gate, g_head_norm, w_out,
             s5_lambda_re, s5_lambda_im, s5_log_dt, s5_b_re, s5_b_im, s5_c_re, s5_c_im,
             s5_d, s5_w_glu, s5_b_glu, *, tm, tb):
    bsz, seq, d = x.shape
    depth = norm_mix.shape[0]
    n = bsz * seq
    tmd = min(2 * tm, n)
    h = x.reshape(n, d)

    wup, cw, cb, wdn = _prep_ffn(ffn_w_up, ffn_conv_w, ffn_conv_b, ffn_w_down, 512)
    c0 = 2 * M_HEADS * M_DK + 2 * M_HEADS * M_DV
    c1 = c0 + 2 * M_HEADS
    c2 = c1 + 2 * G_HEADS * G_DK + 2 * G_HEADS * G_DV
    ng = (c1 - c0) + G_RANK
    w_main = _regroup_cols(w_in, ((0, 0, c0), (c1, c0, c2 - c1)), c0 + c2 - c1, BF16, rows=256)
    w_gate = _regroup_cols(w_in, ((c0, 0, c1 - c0), (c2, c1 - c0, G_RANK), (None, ng, GATE_LANES - ng)),
                           GATE_LANES, BF16, rows=256)
    w_out_b = w_out.astype(BF16)
    w_glu_b = s5_w_glu.astype(BF16)

    for layer in range(depth):
        if layer % 2 == 0:
            e = layer // 2
            h = _even_layer(h, norm_mix[layer], w_main, w_gate, e, m_conv_w[e], m_conv_b[e],
                            m_b_igate[e], m_b_fgate[e], m_head_norm[e], g_w_gate[e], g_b_gate[e],
                            g_head_norm[e], w_out_b, bsz=bsz, seq=seq, tm=tmd, tb=tb)
        else:
            o = layer // 2
            h = _odd_layer(h, norm_mix[layer], s5_lambda_re[o], s5_lambda_im[o], s5_log_dt[o],
                           s5_b_re[o], s5_b_im[o], s5_c_re[o], s5_c_im[o], s5_d[o], w_glu_b, o,
                           s5_b_glu[o], bsz=bsz, seq=seq, tm=tm)
        h = _ffn(h, norm_ffn[layer].reshape(1, d), wup, cw, cb, wdn, layer, seq=seq, tm=tm, tk=512)
    out = _rmsnorm(h, norm_final.reshape(1, d), F32, tm=tm)
    return out.reshape(bsz, seq, d)
```

```python
import functools
import math

import jax
import jax.numpy as jnp
from jax import lax
from jax.experimental import pallas as pl
from jax.experimental.pallas import tpu as pltpu

F32 = jnp.float32
BF16 = jnp.bfloat16
HI = lax.Precision.HIGHEST

EPS = 1e-6
CHUNK = 64
M_HEADS, M_DK, M_DV, M_CONV = 8, 64, 128, 4
G_HEADS, G_DK, G_DV, G_RANK, G_TAU = 4, 128, 256, 16, 16.0
S5_P, S5_N = 16, 64
FFN_CONV = 3
GATE_LANES = 128
SUBCHUNK = 16

VMEM_LIMIT = 56 * 1024 * 1024


def _cparams(*sem):
    return pltpu.CompilerParams(dimension_semantics=sem, vmem_limit_bytes=VMEM_LIMIT)


def _rms(x, g):
    return x * lax.rsqrt(jnp.mean(x * x, axis=-1, keepdims=True) + EPS) * g


def _sigmoid(x):
    return 1.0 / (1.0 + jnp.exp(-x))


def _log_sigmoid(x):
    return jnp.minimum(x, 0.0) - jnp.log(1.0 + jnp.exp(-jnp.abs(x)))


def _dot(a, b, **kw):
    return jnp.dot(a, b, preferred_element_type=F32, **kw)


def _dot_nt(a, b, **kw):
    return lax.dot_general(a, b, (((1,), (1,)), ((), ())), preferred_element_type=F32, **kw)


def _dot_tn(a, b, **kw):
    return lax.dot_general(a, b, (((0,), (0,)), ((), ())), preferred_element_type=F32, **kw)


def _inproj_kernel(x_ref, g_ref, w_ref, wg_ref, o_ref, og_ref, xn_ref):
    @pl.when(pl.program_id(1) == 0)
    def _():
        _rms_rows_to(xn_ref, 0, x_ref, g_ref, x_ref.shape[0])
        og_ref[...] = _dot(xn_ref[...], wg_ref[...])

    o_ref[...] = _dot(xn_ref[...], w_ref[...]).astype(o_ref.dtype)


def _inproj(h, g, w_main, w_gate, e, *, tm, tn):
    n, d = h.shape
    wn = w_main.shape[2]
    return pl.pallas_call(
        _inproj_kernel,
        grid=(n // tm, wn // tn),
        in_specs=[
            pl.BlockSpec((tm, d), lambda i, j: (i, 0)),
            pl.BlockSpec((1, d), lambda i, j: (0, 0)),
            pl.BlockSpec((None, d, tn), lambda i, j: (e, 0, j)),
            pl.BlockSpec((None, d, GATE_LANES), lambda i, j: (e, 0, 0)),
        ],
        out_specs=[
            pl.BlockSpec((tm, tn), lambda i, j: (i, j)),
            pl.BlockSpec((tm, GATE_LANES), lambda i, j: (i, 0)),
        ],
        out_shape=[
            jax.ShapeDtypeStruct((n, wn), BF16),
            jax.ShapeDtypeStruct((n, GATE_LANES), F32),
        ],
        scratch_shapes=[pltpu.VMEM((tm, d), BF16)],
        compiler_params=_cparams("parallel", "arbitrary"),
        name="inproj",
    )(h, g, w_main, w_gate)


def _mlstm_select():
    npair = M_HEADS // 2
    npc = 2 * npair * 2 * CHUNK
    jrow = lax.broadcasted_iota(jnp.int32, (96, npc + M_HEADS * M_DV), 0)
    ncol = lax.broadcasted_iota(jnp.int32, (96, npc + M_HEADS * M_DV), 1)
    half = jrow // 48
    j16 = jrow % 16
    sel_pair = (ncol < npc) & (ncol // (2 * CHUNK) == j16) & ((ncol % (2 * CHUNK)) // CHUNK == half) & (j16 < 8)
    sel_em = (ncol >= npc) & (j16 >= 8) & (j16 < 12) & ((ncol - npc) // M_DV == 2 * (j16 - 8) + half)
    return (sel_pair | sel_em).astype(F32)


def _mlstm_kernel(mq_ref, mk_ref, mv_ref, mo_ref, li_ref, lf_ref, cw_ref, cb_ref, bi_ref, bf_ref, hn_ref,
                  sel_ref, o_ref, qk_scr, qkc_scr, c_scr, m_scr, row_scr, rep_scr, *, tb):
    @pl.when(pl.program_id(1) == 0)
    def _():
        qk_scr[0:8, :] = jnp.zeros((8, 2 * M_HEADS * M_DK), F32)
        c_scr[...] = jnp.zeros(c_scr.shape, F32)
        m_scr[...] = jnp.zeros(m_scr.shape, F32)

    nqk = M_HEADS * M_DK
    qk_scr[8:8 + tb, 0:nqk] = mq_ref[...].astype(F32)
    qk_scr[8:8 + tb, nqk:2 * nqk] = mk_ref[...].astype(F32)
    conv = cb_ref[...] + cw_ref[0:1, :] * qk_scr[5:5 + tb, :]
    for j in range(1, M_CONV):
        conv = conv + cw_ref[j:j + 1, :] * qk_scr[5 + j:5 + j + tb, :]
    tail = qk_scr[tb:tb + 8, :]
    qkc_scr[...] = conv * _sigmoid(conv)
    qk_scr[0:8, :] = tail

    npair = M_HEADS // 2
    lane = lax.broadcasted_iota(jnp.int32, (1, 2 * CHUNK), 1)
    lo_half = lane < CHUNK
    pos = lane % CHUNK
    trow = lax.broadcasted_iota(jnp.int32, (CHUNK, 2 * CHUNK), 0)
    causal2 = (lax.broadcasted_iota(jnp.int32, (CHUNK, 2 * CHUNK), 1) % CHUNK) <= trow
    r2 = lax.broadcasted_iota(jnp.int32, (2 * CHUNK, 2 * CHUNK), 0)
    c2 = lax.broadcasted_iota(jnp.int32, (2 * CHUNK, 2 * CHUNK), 1)
    tri2 = ((r2 // CHUNK == c2 // CHUNK) & (r2 <= c2)).astype(F32)
    row_dk = lax.broadcasted_iota(jnp.int32, (2 * M_DK, 1), 0)
    row_lo = row_dk < M_DK
    ones_v = jnp.ones((CHUNK, M_DV), BF16)
    zeros_v = jnp.zeros((CHUNK, 2 * M_DV), BF16)
    ones_sum = jnp.ones((M_DV, M_DV), BF16)
    ones_n = jnp.ones((CHUNK, M_DV), F32)
    scale = M_DK ** -0.5
    sel = sel_ref[...]

    def exact3(x):
        hi = x.astype(BF16).astype(F32)
        mid = (x - hi).astype(BF16).astype(F32)
        return hi, mid, x - hi - mid

    def half_max(x):
        m0 = jnp.max(jnp.where(lo_half, x, -jnp.inf), axis=-1, keepdims=True)
        m1 = jnp.max(jnp.where(lo_half, -jnp.inf, x), axis=-1, keepdims=True)
        return jnp.where(lo_half, m0, m1)

    nchunk = tb // CHUNK
    li = li_ref[0] + bi_ref[...]
    b = _dot(_log_sigmoid(lf_ref[0] + bf_ref[...]), tri2, precision=HI)
    a = li - b
    cm = a
    for sh in (1, 2, 4, 8, 16, 32):
        cm = jnp.maximum(cm, jnp.where(pos >= sh, pltpu.roll(cm, sh, 1), -jnp.inf))
    b_last = jnp.where(lo_half, b[:, CHUNK - 1:CHUNK], b[:, 2 * CHUNK - 1:2 * CHUNK])
    g = b_last + a
    gmax = half_max(g)
    m = m_scr[...]
    m_starts = []
    for c in range(nchunk):
        m_starts.append(m)
        m = jnp.maximum(b_last[npair * c:npair * (c + 1)] + m, gmax[npair * c:npair * (c + 1)])
    m_scr[...] = m
    m_prev = jnp.concatenate(m_starts, axis=0)
    m_new = jnp.maximum(b_last + m_prev, gmax)
    m_out = b + jnp.maximum(m_prev, cm)
    e1 = b - m_out
    terms = exact3(e1) + exact3(jnp.exp(e1 + m_prev)) + exact3(jnp.exp(-m_out))
    wk = jnp.exp(g - m_new)
    decay = jnp.exp(b_last + m_prev - m_new)
    zero4 = jnp.zeros((npair, 2 * CHUNK), F32)
    for c in range(nchunk):
        rs = slice(npair * c, npair * (c + 1))
        row_scr[c, 0:npair] = a[rs]
        row_scr[c, npair:2 * npair] = wk[rs]
        row_scr[c, 2 * npair:3 * npair] = decay[rs]
        q48 = jnp.concatenate([x for t in range(3) for x in (terms[t][rs], terms[3 + t][rs], terms[6 + t][rs], zero4)],
                              axis=0)
        qfull = jnp.concatenate([q48[:, 0:CHUNK], q48[:, CHUNK:2 * CHUNK]], axis=0)
        rep_scr[c] = _dot_tn(qfull, sel)

    def chunk(c, carry):
        rows = pl.ds(pl.multiple_of(c * CHUNK, CHUNK), CHUNK)
        a = row_scr[c, 0:npair]
        wk = row_scr[c, npair:2 * npair]
        decay = row_scr[c, 2 * npair:3 * npair]
        rep = rep_scr[c]
        for p in range(npair):
            qp = qkc_scr[rows, 2 * M_DK * p:2 * M_DK * (p + 1)] * scale
            kt = qkc_scr[rows, nqk + 2 * M_DK * p:nqk + 2 * M_DK * (p + 1)].T
            kt2 = jnp.concatenate([jnp.where(row_lo, kt, 0.0), jnp.where(row_lo, 0.0, kt)], axis=1)
            e1c = rep[:, 2 * CHUNK * p:2 * CHUNK * (p + 1)]
            wic = rep[:, 2 * CHUNK * (npair + p):2 * CHUNK * (npair + p + 1)]
            dexp = jnp.exp(jnp.where(causal2, a[p:p + 1, :] + e1c, -jnp.inf))
            smat = (_dot(qp.astype(BF16), kt2.astype(BF16)) * dexp).astype(BF16)
            v0 = mv_ref[rows, M_DV * 2 * p:M_DV * (2 * p + 1)]
            v1 = mv_ref[rows, M_DV * (2 * p + 1):M_DV * (2 * p + 2)]
            vbd = jnp.concatenate([jnp.concatenate([v0, ones_v, zeros_v], axis=1),
                                   jnp.concatenate([zeros_v, v1, ones_v], axis=1)], axis=0)
            cbd = c_scr[p]
            numext = _dot(smat, vbd) + _dot((qp * wic).astype(BF16), cbd.astype(BF16))
            for e in range(2):
                h = 2 * p + e
                num = numext[:, 2 * M_DV * e:2 * M_DV * e + M_DV]
                den = numext[:, 2 * M_DV * e + M_DV:2 * M_DV * (e + 1)]
                emc = rep[:, 2 * npair * 2 * CHUNK + M_DV * h:2 * npair * 2 * CHUNK + M_DV * (h + 1)]
                hh = num / jnp.maximum(jnp.abs(den), emc)
                sq = hh * hh
                sq_hi = sq.astype(BF16)
                ms = (_dot(sq_hi, ones_sum) + _dot((sq - sq_hi.astype(F32)).astype(BF16), ones_sum)) * (1.0 / M_DV)
                y = hh * lax.rsqrt(ms + EPS) * hn_ref[:, M_DV * h:M_DV * (h + 1)]
                og = _sigmoid(mo_ref[rows, M_DV * h:M_DV * (h + 1)].astype(F32))
                o_ref[rows, M_DV * h:M_DV * (h + 1)] = (og * y).astype(o_ref.dtype)
            wkp = wk[p:p + 1, :]
            kts = jnp.where(row_lo, kt * wkp[:, 0:CHUNK], kt * wkp[:, CHUNK:2 * CHUNK])
            kts = kts.astype(BF16).astype(F32)
            uc = _dot(kts, jnp.concatenate([v0, v1], axis=1).astype(F32))
            un = _dot(kts, ones_n)
            zc = jnp.zeros((2 * M_DK, M_DV), F32)
            upd = jnp.where(row_lo, jnp.concatenate([uc[:, 0:M_DV], un, zc, zc], axis=1),
                            jnp.concatenate([zc, zc, uc[:, M_DV:2 * M_DV], un], axis=1))
            dp = decay[p:p + 1, :]
            dcol = jnp.where(row_lo, dp[:, 0:1], dp[:, CHUNK:CHUNK + 1])
            c_scr[p] = dcol * cbd + upd
        return carry

    lax.fori_loop(0, tb // CHUNK, chunk, 0)


def _mlstm(pm, gates_t, conv_w, conv_b, bias, hnorm, *, bsz, seq, tb):
    n = pm.shape[0]
    nt = seq // tb
    nqk = M_HEADS * M_DK
    nv = M_HEADS * M_DV
    sel = _mlstm_select()
    rpb = tb // CHUNK * (M_HEADS // 2)
    bias_i = jnp.tile(bias[0], (tb // CHUNK, 1))
    bias_f = jnp.tile(bias[1], (tb // CHUNK, 1))
    return pl.pallas_call(
        functools.partial(_mlstm_kernel, tb=tb),
        grid=(bsz, nt),
        in_specs=[
            pl.BlockSpec((tb, nqk), lambda b, t: (b * nt + t, 0)),
            pl.BlockSpec((tb, nqk), lambda b, t: (b * nt + t, 1)),
            pl.BlockSpec((tb, nv), lambda b, t: (b * nt + t, 1)),
            pl.BlockSpec((tb, nv), lambda b, t: (b * nt + t, 2)),
            pl.BlockSpec((None, 1, rpb, 2 * CHUNK), lambda b, t: (0, b, t, 0)),
            pl.BlockSpec((None, 1, rpb, 2 * CHUNK), lambda b, t: (1, b, t, 0)),
            pl.BlockSpec((M_CONV, 2 * nqk), lambda b, t: (0, 0)),
            pl.BlockSpec((1, 2 * nqk), lambda b, t: (0, 0)),
            pl.BlockSpec((rpb, 2 * CHUNK), lambda b, t: (0, 0)),
            pl.BlockSpec((rpb, 2 * CHUNK), lambda b, t: (0, 0)),
            pl.BlockSpec((1, nv), lambda b, t: (0, 0)),
            pl.BlockSpec(sel.shape, lambda b, t: (0, 0)),
        ],
        out_specs=pl.BlockSpec((tb, nv), lambda b, t: (b * nt + t, 0)),
        out_shape=jax.ShapeDtypeStruct((n, nv), BF16),
        scratch_shapes=[
            pltpu.VMEM((tb + 8, 2 * nqk), F32),
            pltpu.VMEM((tb, 2 * nqk), F32),
            pltpu.VMEM((M_HEADS // 2, 2 * M_DK, 4 * M_DV), F32),
            pltpu.VMEM((M_HEADS // 2, 2 * CHUNK), F32),
            pltpu.VMEM((tb // CHUNK, 2 * M_HEADS, 2 * CHUNK), F32),
            pltpu.VMEM((tb // CHUNK, CHUNK, sel.shape[1]), F32),
        ],
        compiler_params=_cparams("parallel", "arbitrary"),
        name="mlstm",
    )(pm, pm, pm, pm, gates_t, gates_t, conv_w, conv_b, bias_i, bias_f, hnorm, sel)


def _gla_kernel(gq_ref, gk_ref, gv_ref, gg_ref, gc_ref, wg_ref, bg_ref, hn_ref, o_ref, s_scr, *, tb):
    @pl.when(pl.program_id(1) == 0)
    def _():
        s_scr[...] = jnp.zeros(s_scr.shape, F32)

    row = lax.broadcasted_iota(jnp.int32, (CHUNK, CHUNK), 0)
    col = lax.broadcasted_iota(jnp.int32, (CHUNK, CHUNK), 1)
    tril = row >= col
    tri = tril.astype(F32)
    rowk = lax.broadcasted_iota(jnp.int32, (CHUNK, 1), 0)
    nsub = CHUNK // SUBCHUNK
    scale = G_DK ** -0.5

    def chunk(c, carry):
        r0 = pl.multiple_of(c * CHUNK, CHUNK)
        rows = pl.ds(r0, CHUNK)
        pre = _dot(gc_ref[rows, :].astype(BF16), wg_ref[...]) + bg_ref[...]
        la = _log_sigmoid(pre) * (1.0 / G_TAU)
        bc_all = _dot(tri, la, precision=HI)
        for h in range(G_HEADS):
            ks = slice(G_DK * h, G_DK * (h + 1))
            vs = slice(G_DV * h, G_DV * (h + 1))
            bc = bc_all[:, ks]
            q = gq_ref[rows, ks].astype(F32) * scale
            k = gk_ref[rows, ks].astype(F32)
            v = gv_ref[rows, vs]
            st = s_scr[h]
            o = _dot_nt((q * jnp.exp(bc)).astype(BF16), st.astype(BF16))
            cblk = jnp.concatenate(
                [jnp.broadcast_to(bc[SUBCHUNK * i:SUBCHUNK * i + 1, :], (SUBCHUNK, G_DK))
                 for i in range(nsub)], axis=0)
            qt = (q * jnp.exp(bc - cblk)).astype(BF16)
            blocks = []
            for i in range(nsub):
                ci = bc[SUBCHUNK * i:SUBCHUNK * i + 1, :]
                kt = jnp.where(rowk < SUBCHUNK * (i + 1), k * jnp.exp(ci - bc), 0.0).astype(BF16)
                blocks.append(_dot_nt(qt[SUBCHUNK * i:SUBCHUNK * (i + 1), :], kt))
            a = jnp.where(tril, jnp.concatenate(blocks, axis=0), 0.0)
            o = o + _dot(a.astype(BF16), v)
            y = o * lax.rsqrt(jnp.mean(o * o, axis=-1, keepdims=True) + EPS) * hn_ref[:, vs]
            gg = gg_ref[rows, vs].astype(F32)
            o_ref[rows, vs] = (gg * _sigmoid(gg) * y).astype(o_ref.dtype)
            last = bc[CHUNK - 1:CHUNK, :]
            kd = (k * jnp.exp(last - bc)).astype(BF16)
            s_scr[h] = st * jnp.exp(last) + _dot_tn(v, kd)
        return carry

    lax.fori_loop(0, tb // CHUNK, chunk, 0)


def _gla(pm, gates, wg_pad, bg, hnorm, *, bsz, seq, tb):
    n = pm.shape[0]
    nt = seq // tb
    nqk = G_HEADS * G_DK
    nv = G_HEADS * G_DV
    rowmap = lambda b, t: (b * nt + t, 0)
    return pl.pallas_call(
        functools.partial(_gla_kernel, tb=tb),
        grid=(bsz, nt),
        in_specs=[
            pl.BlockSpec((tb, nqk), lambda b, t: (b * nt + t, 6)),
            pl.BlockSpec((tb, nqk), lambda b, t: (b * nt + t, 7)),
            pl.BlockSpec((tb, nv), lambda b, t: (b * nt + t, 4)),
            pl.BlockSpec((tb, nv), lambda b, t: (b * nt + t, 5)),
            pl.BlockSpec((tb, GATE_LANES), rowmap),
            pl.BlockSpec((GATE_LANES, nqk), lambda b, t: (0, 0)),
            pl.BlockSpec((1, nqk), lambda b, t: (0, 0)),
            pl.BlockSpec((1, nv), lambda b, t: (0, 0)),
        ],
        out_specs=pl.BlockSpec((tb, nv), rowmap),
        out_shape=jax.ShapeDtypeStruct((n, nv), BF16),
        scratch_shapes=[pltpu.VMEM((G_HEADS, G_DV, G_DK), F32)],
        compiler_params=_cparams("parallel", "arbitrary"),
        name="gla",
    )(pm, pm, pm, pm, gates, wg_pad, bg, hnorm)


def _outproj_kernel(hm_ref, hg_ref, w1_ref, w2_ref, h_ref, o_ref):
    o_ref[...] = h_ref[...] + _dot(hm_ref[...], w1_ref[...]) + _dot(hg_ref[...], w2_ref[...])


def _outproj(hm, hg, w_out, e, h, *, tm, tn):
    n, d = h.shape
    kh = hm.shape[1]
    return pl.pallas_call(
        _outproj_kernel,
        grid=(n // tm, d // tn),
        in_specs=[
            pl.BlockSpec((tm, kh), lambda i, j: (i, 0)),
            pl.BlockSpec((tm, kh), lambda i, j: (i, 0)),
            pl.BlockSpec((None, kh, tn), lambda i, j: (e, 0, j)),
            pl.BlockSpec((None, kh, tn), lambda i, j: (e, 1, j)),
            pl.BlockSpec((tm, tn), lambda i, j: (i, j)),
        ],
        out_specs=pl.BlockSpec((tm, tn), lambda i, j: (i, j)),
        out_shape=jax.ShapeDtypeStruct((n, d), F32),
        compiler_params=_cparams("parallel", "arbitrary"),
        name="outproj",
    )(hm, hg, w_out, w_out, h)


FFN_HALO = 16
NORM_ROWS = 128
GLU_ROWS = 64


def _rms_rows_to(dst_ref, dst_off, src_ref, g_ref, rows):
    g = g_ref[...]
    for r0 in range(0, rows, NORM_ROWS):
        nr = min(NORM_ROWS, rows - r0)
        dst_ref[dst_off + r0:dst_off + r0 + nr, :] = _rms(src_ref[r0:r0 + nr, :], g).astype(dst_ref.dtype)


def _ffn_kernel(h_ref, halo_ref, g_ref, wup_ref, cwg_ref, cwv_ref, cbg_ref, cbv_ref, wdn_ref,
                o_ref, xn_ref, a_ref, *, tm, tk, rb, tiles_per_seq):
    i = pl.program_id(0)

    @pl.when(pl.program_id(1) == 0)
    def _():
        _rms_rows_to(xn_ref, 0, halo_ref, g_ref, FFN_HALO)
        _rms_rows_to(xn_ref, FFN_HALO, h_ref, g_ref, tm)
        o_ref[...] = h_ref[...]

    w = wup_ref[...]
    wd = wdn_ref[...]
    keep = ((i % tiles_per_seq) != 0).astype(F32)
    a_ref[0:FFN_HALO, :] = _dot(xn_ref[0:FFN_HALO, :], w) * keep
    w0 = jnp.concatenate([cwg_ref[0:1, :], cwv_ref[0:1, :]], axis=-1)
    w1 = jnp.concatenate([cwg_ref[1:2, :], cwv_ref[1:2, :]], axis=-1)
    w2 = jnp.concatenate([cwg_ref[2:3, :], cwv_ref[2:3, :]], axis=-1)
    cb = jnp.concatenate([cbg_ref[...], cbv_ref[...]], axis=-1)
    for r in range(tm // rb):
        r0 = FFN_HALO + r * rb
        a_ref[r0:r0 + rb, :] = _dot(xn_ref[r0:r0 + rb, :], w)
        c = (w2 * a_ref[r0:r0 + rb, :] + w1 * a_ref[r0 - 1:r0 - 1 + rb, :]
             + w0 * a_ref[r0 - 2:r0 - 2 + rb, :] + cb)
        gate = c[:, 0:tk]
        act = (gate * _sigmoid(gate) * c[:, tk:2 * tk]).astype(BF16)
        o_ref[r * rb:(r + 1) * rb, :] += _dot(act, wd)


def _ffn(h, g, wup, cw, cb, wdn, layer, *, seq, tm, tk, rb):
    n, d = h.shape
    nk = wdn.shape[1] // tk
    hb = tm // FFN_HALO
    return pl.pallas_call(
        functools.partial(_ffn_kernel, tm=tm, tk=tk, rb=rb, tiles_per_seq=seq // tm),
        grid=(n // tm, nk),
        in_specs=[
            pl.BlockSpec((tm, d), lambda i, k: (i, 0)),
            pl.BlockSpec((FFN_HALO, d), lambda i, k: (jnp.maximum(i * hb - 1, 0), 0)),
            pl.BlockSpec((1, d), lambda i, k: (0, 0)),
            pl.BlockSpec((None, None, d, 2 * tk), lambda i, k: (layer, k, 0, 0)),
            pl.BlockSpec((None, FFN_CONV, tk), lambda i, k: (layer, 0, k)),
            pl.BlockSpec((None, FFN_CONV, tk), lambda i, k: (layer, 0, nk + k)),
            pl.BlockSpec((None, 1, tk), lambda i, k: (layer, 0, k)),
            pl.BlockSpec((None, 1, tk), lambda i, k: (layer, 0, nk + k)),
            pl.BlockSpec((None, tk, d), lambda i, k: (layer, k, 0)),
        ],
        out_specs=pl.BlockSpec((tm, d), lambda i, k: (i, 0)),
        out_shape=jax.ShapeDtypeStruct((n, d), F32),
        scratch_shapes=[
            pltpu.VMEM((tm + FFN_HALO, d), BF16),
            pltpu.VMEM((tm + FFN_HALO, 2 * tk), F32),
        ],
        compiler_params=_cparams("parallel", "arbitrary"),
        name="convffn",
    )(h, h, g, wup, cw, cw, cb, cb, wdn)


def _rmsnorm_kernel(x_ref, g_ref, o_ref):
    o_ref[...] = _rms(x_ref[...], g_ref[...]).astype(o_ref.dtype)


def _rmsnorm(h, g, dtype, *, tm):
    n, d = h.shape
    return pl.pallas_call(
        _rmsnorm_kernel,
        grid=(n // tm,),
        in_specs=[pl.BlockSpec((tm, d), lambda i: (i, 0)), pl.BlockSpec((1, d), lambda i: (0, 0))],
        out_specs=pl.BlockSpec((tm, d), lambda i: (i, 0)),
        out_shape=jax.ShapeDtypeStruct((n, d), dtype),
        compiler_params=_cparams("parallel"),
        name="rmsnorm",
    )(h, g)


S5_TILE = SUBCHUNK * S5_P
S5_W = CHUNK * S5_P


def _s5_gen_kernel(lr_ref, li_ref, lrc_ref, lic_ref, dt_ref, bre_ref, bim_ref, cre_ref, cim_ref,
                   strip_ref, wzr_ref, wzi_ref, ptr_ref, pti_ref, ar_ref, ai_ref):
    dt = jnp.exp(dt_ref[0])
    lam_r = lr_ref[0]
    lam_i = li_ref[0]
    xr = lam_r * dt
    xi = lam_i * dt
    er = jnp.exp(xr)
    lbr = er * jnp.cos(xi)
    lbi = er * jnp.sin(xi)
    den = lam_r * lam_r + lam_i * lam_i
    cfr = ((lbr - 1.0) * lam_r + lbi * lam_i) / den
    cfi = (lbi * lam_r - (lbr - 1.0) * lam_i) / den
    bbr = cfr * bre_ref[0] - cfi * bim_ref[0]
    bbi = cfr * bim_ref[0] + cfi * bre_ref[0]
    kk = lax.broadcasted_iota(jnp.int32, (CHUNK, S5_N), 0).astype(F32)
    pe = jnp.exp(kk * xr)
    pwr = pe * jnp.cos(kk * xi)
    pwi = pe * jnp.sin(kk * xi)
    for s in range(CHUNK):
        pr = pwr[CHUNK - 1 - s:CHUNK - s, :]
        pi = pwi[CHUNK - 1 - s:CHUNK - s, :]
        wzr_ref[0, S5_P * s:S5_P * (s + 1), :] = (bbr * pr - bbi * pi).astype(wzr_ref.dtype)
        wzi_ref[0, S5_P * s:S5_P * (s + 1), :] = (bbr * pi + bbi * pr).astype(wzi_ref.dtype)
    e64 = jnp.exp(CHUNK * xr)
    ar_ref[0] = e64 * jnp.cos(CHUNK * xi)
    ai_ref[0] = e64 * jnp.sin(CHUNK * xi)
    xrc = lrc_ref[0] * dt
    xic = lic_ref[0] * dt
    tt = lax.broadcasted_iota(jnp.int32, (S5_N, CHUNK), 1).astype(F32)
    pte = jnp.exp(tt * xrc)
    ptr = pte * jnp.cos(tt * xic)
    pti = pte * jnp.sin(tt * xic)
    lane = lax.broadcasted_iota(jnp.int32, (CHUNK, S5_W), 1)
    rep_t = ((lane // S5_P) == lax.broadcasted_iota(jnp.int32, (CHUNK, S5_W), 0)).astype(F32)
    lane_p = lax.broadcasted_iota(jnp.int32, (S5_P, S5_W), 1)
    rep_p = ((lane_p % S5_P) == lax.broadcasted_iota(jnp.int32, (S5_P, S5_W), 0)).astype(F32)
    pr_rep = _dot(ptr, rep_t, precision=HI)
    pi_rep = _dot(pti, rep_t, precision=HI)
    cr_rep = _dot(cre_ref[0], rep_p, precision=HI)
    ci_rep = _dot(cim_ref[0], rep_p, precision=HI)
    q0r = cr_rep * pr_rep - ci_rep * pi_rep
    q0i = cr_rep * pi_rep + ci_rep * pr_rep
    erc = jnp.exp(xrc)
    lbrc = erc * jnp.cos(xic)
    lbic = erc * jnp.sin(xic)
    ptr_ref[0] = (q0r * lbrc - q0i * lbic).astype(ptr_ref.dtype)
    pti_ref[0] = (-(q0r * lbic + q0i * lbrc)).astype(pti_ref.dtype)
    kern = _dot(bbr, q0r, precision=HI) - _dot(bbi, q0i, precision=HI)
    lane_w = lax.broadcasted_iota(jnp.int32, (S5_P, S5_W), 1)
    for s in range(SUBCHUNK):
        blk = kern if s == 0 else jnp.where(lane_w >= S5_P * s, pltpu.roll(kern, S5_P * s, axis=1), 0.0)
        strip_ref[0, S5_P * s:S5_P * (s + 1), :] = blk.astype(strip_ref.dtype)


def _s5_gen(lam_re, lam_im, log_dt, b_re_t, b_im_t, c_re_t, c_im_t):
    g = lam_re.shape[0]
    row3 = lambda a: a.reshape(g, 1, -1)
    col3 = lambda a: a.reshape(g, -1, 1)
    blk = lambda s: pl.BlockSpec((1,) + s, lambda i: (i, 0, 0))
    return pl.pallas_call(
        _s5_gen_kernel,
        grid=(g,),
        in_specs=[blk((1, S5_N)), blk((1, S5_N)), blk((S5_N, 1)), blk((S5_N, 1)), blk((1, 1)),
                  blk((S5_P, S5_N)), blk((S5_P, S5_N)), blk((S5_N, S5_P)), blk((S5_N, S5_P))],
        out_specs=[blk((S5_TILE, S5_W)), blk((S5_W, S5_N)), blk((S5_W, S5_N)),
                   blk((S5_N, S5_W)), blk((S5_N, S5_W)), blk((1, S5_N)), blk((1, S5_N))],
        out_shape=[
            jax.ShapeDtypeStruct((g, S5_TILE, S5_W), BF16),
            jax.ShapeDtypeStruct((g, S5_W, S5_N), BF16),
            jax.ShapeDtypeStruct((g, S5_W, S5_N), BF16),
            jax.ShapeDtypeStruct((g, S5_N, S5_W), BF16),
            jax.ShapeDtypeStruct((g, S5_N, S5_W), BF16),
            jax.ShapeDtypeStruct((g, 1, S5_N), F32),
            jax.ShapeDtypeStruct((g, 1, S5_N), F32),
        ],
        compiler_params=_cparams("parallel"),
        name="s5_gen",
    )(row3(lam_re), row3(lam_im), col3(lam_re), col3(lam_im), log_dt.reshape(g, 1, 1),
      b_re_t, b_im_t, c_re_t, c_im_t)


def _s5_apply_kernel(u_ref, strip_ref, wzr_ref, wzi_ref, ptr_ref, pti_ref, ar_ref, ai_ref,
                     y_ref, zr_scr, zi_scr, xr_scr, xi_scr, *, bsz, nchunks):
    u = u_ref[0]
    zr_scr[...] = _dot(u, wzr_ref[0])
    zi_scr[...] = _dot(u, wzi_ref[0])
    a_r = ar_ref[0]
    a_i = ai_ref[0]

    def step(c, carry):
        x_r, x_i = carry
        rows = pl.ds(pl.multiple_of(c * bsz, bsz), bsz)
        xr_scr[rows, :] = x_r
        xi_scr[rows, :] = x_i
        n_r = a_r * x_r - a_i * x_i + zr_scr[rows, :]
        n_i = a_r * x_i + a_i * x_r + zi_scr[rows, :]
        return n_r, n_i

    zero = jnp.zeros((bsz, S5_N), F32)
    lax.fori_loop(0, nchunks, step, (zero, zero))
    xr = xr_scr[...].astype(BF16)
    xi = xi_scr[...].astype(BF16)
    nt = S5_W // S5_TILE
    for j in range(nt):
        cols = slice(S5_TILE * j, S5_TILE * (j + 1))
        acc = _dot(xr, ptr_ref[0, :, cols]) + _dot(xi, pti_ref[0, :, cols])
        for i in range(j + 1):
            acc = acc + _dot(u[:, S5_TILE * i:S5_TILE * (i + 1)],
                             strip_ref[0, :, S5_TILE * (j - i):S5_TILE * (j - i + 1)])
        y_ref[0, :, cols] = acc.astype(y_ref.dtype)


def _s5_apply(ut, strip, wzr, wzi, ptr, pti, ar, ai, *, bsz, nchunks):
    g, rows, _ = ut.shape
    blk = lambda s: pl.BlockSpec((1,) + s, lambda i: (i, 0, 0))
    return pl.pallas_call(
        functools.partial(_s5_apply_kernel, bsz=bsz, nchunks=nchunks),
        grid=(g,),
        in_specs=[blk((rows, S5_W)), blk((S5_TILE, S5_W)), blk((S5_W, S5_N)), blk((S5_W, S5_N)),
                  blk((S5_N, S5_W)), blk((S5_N, S5_W)), blk((1, S5_N)), blk((1, S5_N))],
        out_specs=blk((rows, S5_W)),
        out_shape=jax.ShapeDtypeStruct((g, rows, S5_W), BF16),
        scratch_shapes=[pltpu.VMEM((rows, S5_N), F32)] * 4,
        compiler_params=_cparams("parallel"),
        name="s5_apply",
    )(ut, strip, wzr, wzi, ptr, pti, ar, ai)


def _s5_glu_kernel(h_ref, y_ref, g_ref, d_ref, w_ref, b_ref, o_ref, yv_ref, yb_ref, *, tn):
    j = pl.program_id(1)

    @pl.when(j == 0)
    def _():
        g = g_ref[...]
        dv = d_ref[...]

        def body(b, carry):
            rs = pl.ds(pl.multiple_of(b * GLU_ROWS, GLU_ROWS), GLU_ROWS)
            y = y_ref[rs, :].astype(F32) + dv * _rms(h_ref[rs, :], g)
            y = 0.5 * y * (1.0 + jnp.tanh(math.sqrt(2.0 / math.pi) * (y + 0.044715 * (y * y * y))))
            yv_ref[rs, :] = y
            yb_ref[rs, :] = y.astype(BF16)
            return carry

        lax.fori_loop(0, h_ref.shape[0] // GLU_ROWS, body, 0)

    cols = pl.ds(pl.multiple_of(j * tn, tn), tn)
    z = _dot(yb_ref[...], w_ref[...]) + b_ref[...]
    o_ref[...] = h_ref[:, cols] + yv_ref[:, cols] * _sigmoid(z)


def _s5_glu(h, y, g, dvec, w_glu, o, b_glu, *, tm, tn):
    n, d = h.shape
    return pl.pallas_call(
        functools.partial(_s5_glu_kernel, tn=tn),
        grid=(n // tm, d // tn),
        in_specs=[
            pl.BlockSpec((tm, d), lambda i, j: (i, 0)),
            pl.BlockSpec((tm, d), lambda i, j: (i, 0)),
            pl.BlockSpec((1, d), lambda i, j: (0, 0)),
            pl.BlockSpec((1, d), lambda i, j: (0, 0)),
            pl.BlockSpec((None, d, tn), lambda i, j: (o, 0, j)),
            pl.BlockSpec((1, tn), lambda i, j: (0, j)),
        ],
        out_specs=pl.BlockSpec((tm, tn), lambda i, j: (i, j)),
        out_shape=jax.ShapeDtypeStruct((n, d), F32),
        scratch_shapes=[pltpu.VMEM((tm, d), F32), pltpu.VMEM((tm, d), BF16)],
        compiler_params=_cparams("parallel", "arbitrary"),
        name="s5_glu",
    )(h, y, g, dvec, w_glu, b_glu)


def _even_layer(h, g_mix, w_main, w_gate, e, m_conv_w, m_conv_b, m_b_igate, m_b_fgate, m_head_norm,
                g_w_gate, g_b_gate, g_head_norm, w_out, *, bsz, seq, tm, tb):
    n, d = h.shape
    pm, gates = _inproj(h, g_mix.reshape(1, d), w_main, w_gate, e, tm=tm, tn=1024)
    gates_t = gates[:, 0:2 * M_HEADS].reshape(bsz, seq // CHUNK, CHUNK, 2, M_HEADS).transpose(3, 0, 1, 4, 2)
    gates_t = gates_t.reshape(2, bsz, seq // CHUNK * (M_HEADS // 2), 2 * CHUNK)
    bias = jnp.repeat(jnp.stack([m_b_igate, m_b_fgate]).astype(F32), CHUNK, axis=-1)
    bias = bias.reshape(2, M_HEADS // 2, 2 * CHUNK)
    hm = _mlstm(pm, gates_t, m_conv_w, m_conv_b.reshape(1, -1), bias, m_head_norm.reshape(1, -1),
                bsz=bsz, seq=seq, tb=tb)
    wg_pad = jnp.zeros((GATE_LANES, G_HEADS * G_DK), BF16).at[16:16 + G_RANK].set(g_w_gate.astype(BF16))
    hg = _gla(pm, gates, wg_pad, g_b_gate.reshape(1, -1), g_head_norm.reshape(1, -1),
              bsz=bsz, seq=seq, tb=tb)
    return _outproj(hm, hg, w_out, e, h, tm=tm, tn=1024)


def _odd_layer(h, g_mix, lam_re, lam_im, log_dt, b_re, b_im, c_re, c_im, dvec, w_glu, o, b_glu,
               *, bsz, seq, tm):
    n, d = h.shape
    groups = d // S5_P
    nchunks = seq // CHUNK
    ops = _s5_gen(lam_re, lam_im, log_dt, jnp.swapaxes(b_re, 1, 2), jnp.swapaxes(b_im, 1, 2),
                  jnp.swapaxes(c_re, 1, 2), jnp.swapaxes(c_im, 1, 2))
    u = _rmsnorm(h, g_mix.reshape(1, d), BF16, tm=tm)
    ut = u.reshape(bsz, nchunks, CHUNK, groups, S5_P).transpose(3, 1, 0, 2, 4)
    ut = ut.reshape(groups, nchunks * bsz, S5_W)
    yt = _s5_apply(ut, *ops, bsz=bsz, nchunks=nchunks)
    y = yt.reshape(groups, nchunks, bsz, CHUNK, S5_P).transpose(2, 1, 3, 0, 4).reshape(n, d)
    return _s5_glu(h, y, g_mix.reshape(1, d), dvec.reshape(1, d), w_glu, o, b_glu.reshape(1, d),
                   tm=tm, tn=1024)


def _regroup_kernel(w_ref, o_ref, *, segments):
    for src, dst, n in segments:
        if src is None:
            o_ref[:, dst:dst + n] = jnp.zeros((o_ref.shape[0], n), o_ref.dtype)
        else:
            o_ref[:, dst:dst + n] = w_ref[:, src:src + n].astype(o_ref.dtype)


def _regroup_cols(w, segments, width, dtype, *, rows):
    nl, r, c = w.shape
    return pl.pallas_call(
        functools.partial(_regroup_kernel, segments=segments),
        grid=(nl, r // rows),
        in_specs=[pl.BlockSpec((None, rows, c), lambda l, i: (l, i, 0))],
        out_specs=pl.BlockSpec((None, rows, width), lambda l, i: (l, i, 0)),
        out_shape=jax.ShapeDtypeStruct((nl, r, width), dtype),
        compiler_params=_cparams("parallel", "parallel"),
        name="regroup_cols",
    )(w)


def _tile_up_kernel(w_ref, o_ref, *, dff, tk):
    for k in range(o_ref.shape[0]):
        n = min(tk, dff - k * tk)
        for half, base in ((0, 0), (1, dff)):
            o_ref[k, :, half * tk:half * tk + n] = w_ref[:, base + k * tk:base + k * tk + n].astype(o_ref.dtype)
            if n < tk:
                o_ref[k, :, half * tk + n:(half + 1) * tk] = jnp.zeros((o_ref.shape[1], tk - n), o_ref.dtype)


def _tile_up(w, tk, *, rows):
    nl, d, two_ff = w.shape
    dff = two_ff // 2
    nk = -(-dff // tk)
    return pl.pallas_call(
        functools.partial(_tile_up_kernel, dff=dff, tk=tk),
        grid=(nl, d // rows),
        in_specs=[pl.BlockSpec((None, rows, two_ff), lambda l, i: (l, i, 0))],
        out_specs=pl.BlockSpec((None, nk, rows, 2 * tk), lambda l, i: (l, 0, i, 0)),
        out_shape=jax.ShapeDtypeStruct((nl, nk, d, 2 * tk), BF16),
        compiler_params=_cparams("parallel", "parallel"),
        name="tile_up",
    )(w)


def _prep_ffn(ffn_w_up, ffn_conv_w, ffn_conv_b, ffn_w_down, tk):
    dff = ffn_w_down.shape[1]
    pad = -dff % tk
    dffp = dff + pad

    def padded(a, dtype):
        lead = a.shape[:-1]
        gv = jnp.pad(a.reshape(lead + (2, dff)), [(0, 0)] * len(lead) + [(0, 0), (0, pad)])
        return gv.astype(dtype).reshape(lead + (2 * dffp,))

    wup = _tile_up(ffn_w_up, tk, rows=128)
    cw = padded(ffn_conv_w, F32)
    cb = padded(ffn_conv_b, F32)[:, None, :]
    wdn = jnp.pad(ffn_w_down, [(0, 0), (0, pad), (0, 0)]).astype(BF16)
    return wup, cw, cb, wdn


def kernel(x, norm_mix, norm_ffn, ffn_w_up, ffn_conv_w, ffn_conv_b, ffn_w_down, norm_final,
           w_in, m_conv_w, m_conv_b, m_b_igate, m_b_fgate, m_head_norm,
           g_w_gate, g_b_gate, g_head_norm, w_out,
           s5_lambda_re, s5_lambda_im, s5_log_dt, s5_b_re, s5_b_im, s5_c_re, s5_c_im,
           s5_d, s5_w_glu, s5_b_glu):
    return _forward(x, norm_mix, norm_ffn, ffn_w_up, ffn_conv_w, ffn_conv_b, ffn_w_down, norm_final,
                    w_in, m_conv_w, m_conv_b, m_b_igate, m_b_fgate, m_head_norm,
                    g_w_gate, g_b_gate, g_head_norm, w_out,
                    s5_lambda_re, s5_lambda_im, s5_log_dt, s5_b_re, s5_b_im, s5_c_re, s5_c_im,
                    s5_d, s5_w_glu, s5_b_glu, tm=512, tf=512, tb=512)


def _forward(x, norm_mix, norm_ffn, ffn_w_up, ffn_conv_w, ffn_conv_b, ffn_w_down, norm_final,
             w_in, m_conv_w, m_conv_b, m_b_igate, m_b_fgate, m_head_norm,
             g_w_gate, g_b_gate, g_head_norm, w_out,
             s5_lambda_re, s5_lambda_im, s5_log_dt, s5_b_re, s5_b_im, s5_c_re, s5_c_im,
             s5_d, s5_w_glu, s5_b_glu, *, tm, tf, tb):
    bsz, seq, d = x.shape
    depth = norm_mix.shape[0]
    n = bsz * seq
    tmd = min(2 * tm, n)
    h = x.reshape(n, d)

    wup, cw, cb, wdn = _prep_ffn(ffn_w_up, ffn_conv_w, ffn_conv_b, ffn_w_down, 512)
    c0 = 2 * M_HEADS * M_DK + 2 * M_HEADS * M_DV
    c1 = c0 + 2 * M_HEADS
    c2 = c1 + 2 * G_HEADS * G_DK + 2 * G_HEADS * G_DV
    ng = (c1 - c0) + G_RANK
    w_main = _regroup_cols(w_in, ((0, 0, c0), (c1, c0, c2 - c1)), c0 + c2 - c1, BF16, rows=256)
    w_gate = _regroup_cols(w_in, ((c0, 0, c1 - c0), (c2, c1 - c0, G_RANK), (None, ng, GATE_LANES - ng)),
                           GATE_LANES, BF16, rows=256)
    w_out_b = w_out.astype(BF16)
    w_glu_b = s5_w_glu.astype(BF16)

    for layer in range(depth):
        if layer % 2 == 0:
            e = layer // 2
            h = _even_layer(h, norm_mix[layer], w_main, w_gate, e, m_conv_w[e], m_conv_b[e],
                            m_b_igate[e], m_b_fgate[e], m_head_norm[e], g_w_gate[e], g_b_gate[e],
                            g_head_norm[e], w_out_b, bsz=bsz, seq=seq, tm=tmd, tb=tb)
        else:
            o = layer // 2
            h = _odd_layer(h, norm_mix[layer], s5_lambda_re[o], s5_lambda_im[o], s5_log_dt[o],
                           s5_b_re[o], s5_b_im[o], s5_c_re[o], s5_c_im[o], s5_d[o], w_glu_b, o,
                           s5_b_glu[o], bsz=bsz, seq=seq, tm=tm)
        h = _ffn(h, norm_ffn[layer].reshape(1, d), wup, cw, cb, wdn, layer, seq=seq, tm=tf, tk=512,
                 rb=min(256, tf))
    out = _rmsnorm(h, norm_final.reshape(1, d), F32, tm=tm)
    return out.reshape(bsz, seq, d)
```

```python
import functools
import math

import jax
import jax.numpy as jnp
from jax import lax
from jax.experimental import pallas as pl
from jax.experimental.pallas import tpu as pltpu

F32 = jnp.float32
BF16 = jnp.bfloat16
HI = lax.Precision.HIGHEST

EPS = 1e-6
CHUNK = 64
M_HEADS, M_DK, M_DV, M_CONV = 8, 64, 128, 4
G_HEADS, G_DK, G_DV, G_RANK, G_TAU = 4, 128, 256, 16, 16.0
S5_P, S5_N = 16, 64
FFN_CONV = 3
GATE_LANES = 128
SUBCHUNK = 16

VMEM_LIMIT = 56 * 1024 * 1024


def _cparams(*sem):
    return pltpu.CompilerParams(dimension_semantics=sem, vmem_limit_bytes=VMEM_LIMIT)


def _rms(x, g):
    return x * lax.rsqrt(jnp.mean(x * x, axis=-1, keepdims=True) + EPS) * g


def _sigmoid(x):
    return 1.0 / (1.0 + jnp.exp(-x))


def _log_sigmoid(x):
    return jnp.minimum(x, 0.0) - jnp.log(1.0 + jnp.exp(-jnp.abs(x)))


def _dot(a, b, **kw):
    return jnp.dot(a, b, preferred_element_type=F32, **kw)


def _dot_nt(a, b, **kw):
    return lax.dot_general(a, b, (((1,), (1,)), ((), ())), preferred_element_type=F32, **kw)


def _dot_tn(a, b, **kw):
    return lax.dot_general(a, b, (((0,), (0,)), ((), ())), preferred_element_type=F32, **kw)


def _inproj_kernel(x_ref, g_ref, w_ref, wg_ref, o_ref, og_ref, xn_ref):
    @pl.when(pl.program_id(1) == 0)
    def _():
        _rms_rows_to(xn_ref, 0, x_ref, g_ref, x_ref.shape[0])
        og_ref[...] = _dot(xn_ref[...], wg_ref[...])

    o_ref[...] = _dot(xn_ref[...], w_ref[...]).astype(o_ref.dtype)


def _inproj(h, g, w_main, w_gate, e, *, tm, tn):
    n, d = h.shape
    wn = w_main.shape[2]
    return pl.pallas_call(
        _inproj_kernel,
        grid=(n // tm, wn // tn),
        in_specs=[
            pl.BlockSpec((tm, d), lambda i, j: (i, 0)),
            pl.BlockSpec((1, d), lambda i, j: (0, 0)),
            pl.BlockSpec((None, d, tn), lambda i, j: (e, 0, j)),
            pl.BlockSpec((None, d, GATE_LANES), lambda i, j: (e, 0, 0)),
        ],
        out_specs=[
            pl.BlockSpec((tm, tn), lambda i, j: (i, j)),
            pl.BlockSpec((tm, GATE_LANES), lambda i, j: (i, 0)),
        ],
        out_shape=[
            jax.ShapeDtypeStruct((n, wn), BF16),
            jax.ShapeDtypeStruct((n, GATE_LANES), F32),
        ],
        scratch_shapes=[pltpu.VMEM((tm, d), BF16)],
        compiler_params=_cparams("parallel", "arbitrary"),
        name="inproj",
    )(h, g, w_main, w_gate)


def _mlstm_select():
    npair = M_HEADS // 2
    npc = 2 * npair * 2 * CHUNK
    jrow = lax.broadcasted_iota(jnp.int32, (96, npc + M_HEADS * M_DV), 0)
    ncol = lax.broadcasted_iota(jnp.int32, (96, npc + M_HEADS * M_DV), 1)
    half = jrow // 48
    j16 = jrow % 16
    sel_pair = (ncol < npc) & (ncol // (2 * CHUNK) == j16) & ((ncol % (2 * CHUNK)) // CHUNK == half) & (j16 < 8)
    sel_em = (ncol >= npc) & (j16 >= 8) & (j16 < 12) & ((ncol - npc) // M_DV == 2 * (j16 - 8) + half)
    return (sel_pair | sel_em).astype(F32)


def _mlstm_kernel(mq_ref, mk_ref, mv_ref, mo_ref, li_ref, lf_ref, cw_ref, cb_ref, bi_ref, bf_ref, hn_ref,
                  sel_ref, o_ref, qk_scr, qkc_scr, c_scr, m_scr, row_scr, rep_scr, *, tb):
    @pl.when(pl.program_id(1) == 0)
    def _():
        qk_scr[0:8, :] = jnp.zeros((8, 2 * M_HEADS * M_DK), F32)
        c_scr[...] = jnp.zeros(c_scr.shape, F32)
        m_scr[...] = jnp.zeros(m_scr.shape, F32)

    nqk = M_HEADS * M_DK
    qk_scr[8:8 + tb, 0:nqk] = mq_ref[...].astype(F32)
    qk_scr[8:8 + tb, nqk:2 * nqk] = mk_ref[...].astype(F32)
    conv = cb_ref[...] + cw_ref[0:1, :] * qk_scr[5:5 + tb, :]
    for j in range(1, M_CONV):
        conv = conv + cw_ref[j:j + 1, :] * qk_scr[5 + j:5 + j + tb, :]
    tail = qk_scr[tb:tb + 8, :]
    qkc_scr[...] = conv * _sigmoid(conv)
    qk_scr[0:8, :] = tail

    npair = M_HEADS // 2
    lane = lax.broadcasted_iota(jnp.int32, (1, 2 * CHUNK), 1)
    lo_half = lane < CHUNK
    pos = lane % CHUNK
    trow = lax.broadcasted_iota(jnp.int32, (CHUNK, 2 * CHUNK), 0)
    causal2 = (lax.broadcasted_iota(jnp.int32, (CHUNK, 2 * CHUNK), 1) % CHUNK) <= trow
    r2 = lax.broadcasted_iota(jnp.int32, (2 * CHUNK, 2 * CHUNK), 0)
    c2 = lax.broadcasted_iota(jnp.int32, (2 * CHUNK, 2 * CHUNK), 1)
    tri2 = ((r2 // CHUNK == c2 // CHUNK) & (r2 <= c2)).astype(F32)
    row_dk = lax.broadcasted_iota(jnp.int32, (2 * M_DK, 1), 0)
    row_lo = row_dk < M_DK
    ones_v = jnp.ones((CHUNK, M_DV), BF16)
    zeros_v = jnp.zeros((CHUNK, 2 * M_DV), BF16)
    ones_sum = jnp.ones((M_DV, M_DV), BF16)
    ones_n = jnp.ones((CHUNK, M_DV), F32)
    scale = M_DK ** -0.5
    sel = sel_ref[...]

    def exact3(x):
        hi = x.astype(BF16).astype(F32)
        mid = (x - hi).astype(BF16).astype(F32)
        return hi, mid, x - hi - mid

    def half_max(x):
        m0 = jnp.max(jnp.where(lo_half, x, -jnp.inf), axis=-1, keepdims=True)
        m1 = jnp.max(jnp.where(lo_half, -jnp.inf, x), axis=-1, keepdims=True)
        return jnp.where(lo_half, m0, m1)

    nchunk = tb // CHUNK
    li = li_ref[0] + bi_ref[...]
    b = _dot(_log_sigmoid(lf_ref[0] + bf_ref[...]), tri2, precision=HI)
    a = li - b
    cm = a
    for sh in (1, 2, 4, 8, 16, 32):
        cm = jnp.maximum(cm, jnp.where(pos >= sh, pltpu.roll(cm, sh, 1), -jnp.inf))
    b_last = jnp.where(lo_half, b[:, CHUNK - 1:CHUNK], b[:, 2 * CHUNK - 1:2 * CHUNK])
    g = b_last + a
    gmax = half_max(g)
    m = m_scr[...]
    m_starts = []
    for c in range(nchunk):
        m_starts.append(m)
        m = jnp.maximum(b_last[npair * c:npair * (c + 1)] + m, gmax[npair * c:npair * (c + 1)])
    m_scr[...] = m
    m_prev = jnp.concatenate(m_starts, axis=0)
    m_new = jnp.maximum(b_last + m_prev, gmax)
    m_out = b + jnp.maximum(m_prev, cm)
    e1 = b - m_out
    terms = exact3(e1) + exact3(jnp.exp(e1 + m_prev)) + exact3(jnp.exp(-m_out))
    wk = jnp.exp(g - m_new)
    decay = jnp.exp(b_last + m_prev - m_new)
    zero4 = jnp.zeros((npair, 2 * CHUNK), F32)
    for c in range(nchunk):
        rs = slice(npair * c, npair * (c + 1))
        row_scr[c, 0:npair] = a[rs]
        row_scr[c, npair:2 * npair] = wk[rs]
        row_scr[c, 2 * npair:3 * npair] = decay[rs]
        q48 = jnp.concatenate([x for t in range(3) for x in (terms[t][rs], terms[3 + t][rs], terms[6 + t][rs], zero4)],
                              axis=0)
        qfull = jnp.concatenate([q48[:, 0:CHUNK], q48[:, CHUNK:2 * CHUNK]], axis=0)
        rep_scr[c] = _dot_tn(qfull, sel)

    def chunk(c, carry):
        rows = pl.ds(pl.multiple_of(c * CHUNK, CHUNK), CHUNK)
        a = row_scr[c, 0:npair]
        wk = row_scr[c, npair:2 * npair]
        decay = row_scr[c, 2 * npair:3 * npair]
        rep = rep_scr[c]
        for p in range(npair):
            qp = qkc_scr[rows, 2 * M_DK * p:2 * M_DK * (p + 1)] * scale
            kt = qkc_scr[rows, nqk + 2 * M_DK * p:nqk + 2 * M_DK * (p + 1)].T
            kt2 = jnp.concatenate([jnp.where(row_lo, kt, 0.0), jnp.where(row_lo, 0.0, kt)], axis=1)
            e1c = rep[:, 2 * CHUNK * p:2 * CHUNK * (p + 1)]
            wic = rep[:, 2 * CHUNK * (npair + p):2 * CHUNK * (npair + p + 1)]
            dexp = jnp.exp(jnp.where(causal2, a[p:p + 1, :] + e1c, -jnp.inf))
            smat = (_dot(qp.astype(BF16), kt2.astype(BF16)) * dexp).astype(BF16)
            v0 = mv_ref[rows, M_DV * 2 * p:M_DV * (2 * p + 1)]
            v1 = mv_ref[rows, M_DV * (2 * p + 1):M_DV * (2 * p + 2)]
            vbd = jnp.concatenate([jnp.concatenate([v0, ones_v, zeros_v], axis=1),
                                   jnp.concatenate([zeros_v, v1, ones_v], axis=1)], axis=0)
            cbd = c_scr[p]
            numext = _dot(smat, vbd) + _dot((qp * wic).astype(BF16), cbd.astype(BF16))
            for e in range(2):
                h = 2 * p + e
                num = numext[:, 2 * M_DV * e:2 * M_DV * e + M_DV]
                den = numext[:, 2 * M_DV * e + M_DV:2 * M_DV * (e + 1)]
                emc = rep[:, 2 * npair * 2 * CHUNK + M_DV * h:2 * npair * 2 * CHUNK + M_DV * (h + 1)]
                hh = num / jnp.maximum(jnp.abs(den), emc)
                sq = hh * hh
                sq_hi = sq.astype(BF16)
                ms = (_dot(sq_hi, ones_sum) + _dot((sq - sq_hi.astype(F32)).astype(BF16), ones_sum)) * (1.0 / M_DV)
                y = hh * lax.rsqrt(ms + EPS) * hn_ref[:, M_DV * h:M_DV * (h + 1)]
                og = _sigmoid(mo_ref[rows, M_DV * h:M_DV * (h + 1)].astype(F32))
                o_ref[rows, M_DV * h:M_DV * (h + 1)] = (og * y).astype(o_ref.dtype)
            wkp = wk[p:p + 1, :]
            kts = jnp.where(row_lo, kt * wkp[:, 0:CHUNK], kt * wkp[:, CHUNK:2 * CHUNK])
            kts = kts.astype(BF16).astype(F32)
            uc = _dot(kts, jnp.concatenate([v0, v1], axis=1).astype(F32))
            un = _dot(kts, ones_n)
            zc = jnp.zeros((2 * M_DK, M_DV), F32)
            upd = jnp.where(row_lo, jnp.concatenate([uc[:, 0:M_DV], un, zc, zc], axis=1),
                            jnp.concatenate([zc, zc, uc[:, M_DV:2 * M_DV], un], axis=1))
            dp = decay[p:p + 1, :]
            dcol = jnp.where(row_lo, dp[:, 0:1], dp[:, CHUNK:CHUNK + 1])
            c_scr[p] = dcol * cbd + upd
        return carry

    lax.fori_loop(0, tb // CHUNK, chunk, 0)


def _mlstm(pm, gates_t, conv_w, conv_b, bias, hnorm, *, bsz, seq, tb):
    n = pm.shape[0]
    nt = seq // tb
    nqk = M_HEADS * M_DK
    nv = M_HEADS * M_DV
    sel = _mlstm_select()
    rpb = tb // CHUNK * (M_HEADS // 2)
    bias_i = jnp.tile(bias[0], (tb // CHUNK, 1))
    bias_f = jnp.tile(bias[1], (tb // CHUNK, 1))
    return pl.pallas_call(
        functools.partial(_mlstm_kernel, tb=tb),
        grid=(bsz, nt),
        in_specs=[
            pl.BlockSpec((tb, nqk), lambda b, t: (b * nt + t, 0)),
            pl.BlockSpec((tb, nqk), lambda b, t: (b * nt + t, 1)),
            pl.BlockSpec((tb, nv), lambda b, t: (b * nt + t, 1)),
            pl.BlockSpec((tb, nv), lambda b, t: (b * nt + t, 2)),
            pl.BlockSpec((None, 1, rpb, 2 * CHUNK), lambda b, t: (0, b, t, 0)),
            pl.BlockSpec((None, 1, rpb, 2 * CHUNK), lambda b, t: (1, b, t, 0)),
            pl.BlockSpec((M_CONV, 2 * nqk), lambda b, t: (0, 0)),
            pl.BlockSpec((1, 2 * nqk), lambda b, t: (0, 0)),
            pl.BlockSpec((rpb, 2 * CHUNK), lambda b, t: (0, 0)),
            pl.BlockSpec((rpb, 2 * CHUNK), lambda b, t: (0, 0)),
            pl.BlockSpec((1, nv), lambda b, t: (0, 0)),
            pl.BlockSpec(sel.shape, lambda b, t: (0, 0)),
        ],
        out_specs=pl.BlockSpec((tb, nv), lambda b, t: (b * nt + t, 0)),
        out_shape=jax.ShapeDtypeStruct((n, nv), BF16),
        scratch_shapes=[
            pltpu.VMEM((tb + 8, 2 * nqk), F32),
            pltpu.VMEM((tb, 2 * nqk), F32),
            pltpu.VMEM((M_HEADS // 2, 2 * M_DK, 4 * M_DV), F32),
            pltpu.VMEM((M_HEADS // 2, 2 * CHUNK), F32),
            pltpu.VMEM((tb // CHUNK, 2 * M_HEADS, 2 * CHUNK), F32),
            pltpu.VMEM((tb // CHUNK, CHUNK, sel.shape[1]), F32),
        ],
        compiler_params=_cparams("parallel", "arbitrary"),
        name="mlstm",
    )(pm, pm, pm, pm, gates_t, gates_t, conv_w, conv_b, bias_i, bias_f, hnorm, sel)


def _gla_kernel(gq_ref, gk_ref, gv_ref, gg_ref, gc_ref, wg_ref, bg_ref, hn_ref, o_ref, s_scr, *, tb):
    @pl.when(pl.program_id(1) == 0)
    def _():
        s_scr[...] = jnp.zeros(s_scr.shape, F32)

    row = lax.broadcasted_iota(jnp.int32, (CHUNK, CHUNK), 0)
    col = lax.broadcasted_iota(jnp.int32, (CHUNK, CHUNK), 1)
    tril = row >= col
    tri = tril.astype(F32)
    rowk = lax.broadcasted_iota(jnp.int32, (CHUNK, 1), 0)
    nsub = CHUNK // SUBCHUNK
    scale = G_DK ** -0.5

    def chunk(c, carry):
        r0 = pl.multiple_of(c * CHUNK, CHUNK)
        rows = pl.ds(r0, CHUNK)
        pre = _dot(gc_ref[rows, :].astype(BF16), wg_ref[...]) + bg_ref[...]
        la = _log_sigmoid(pre) * (1.0 / G_TAU)
        bc_all = _dot(tri, la, precision=HI)
        for h in range(G_HEADS):
            ks = slice(G_DK * h, G_DK * (h + 1))
            vs = slice(G_DV * h, G_DV * (h + 1))
            bc = bc_all[:, ks]
            q = gq_ref[rows, ks].astype(F32) * scale
            k = gk_ref[rows, ks].astype(F32)
            v = gv_ref[rows, vs]
            st = s_scr[h]
            o = _dot_nt((q * jnp.exp(bc)).astype(BF16), st.astype(BF16))
            cblk = jnp.concatenate(
                [jnp.broadcast_to(bc[SUBCHUNK * i:SUBCHUNK * i + 1, :], (SUBCHUNK, G_DK))
                 for i in range(nsub)], axis=0)
            qt = (q * jnp.exp(bc - cblk)).astype(BF16)
            blocks = []
            for i in range(nsub):
                ci = bc[SUBCHUNK * i:SUBCHUNK * i + 1, :]
                kt = jnp.where(rowk < SUBCHUNK * (i + 1), k * jnp.exp(ci - bc), 0.0).astype(BF16)
                blocks.append(_dot_nt(qt[SUBCHUNK * i:SUBCHUNK * (i + 1), :], kt))
            a = jnp.where(tril, jnp.concatenate(blocks, axis=0), 0.0)
            o = o + _dot(a.astype(BF16), v)
            y = o * lax.rsqrt(jnp.mean(o * o, axis=-1, keepdims=True) + EPS) * hn_ref[:, vs]
            gg = gg_ref[rows, vs].astype(F32)
            o_ref[rows, vs] = (gg * _sigmoid(gg) * y).astype(o_ref.dtype)
            last = bc[CHUNK - 1:CHUNK, :]
            kd = (k * jnp.exp(last - bc)).astype(BF16)
            s_scr[h] = st * jnp.exp(last) + _dot_tn(v, kd)
        return carry

    lax.fori_loop(0, tb // CHUNK, chunk, 0)


def _gla(pm, gates, wg_pad, bg, hnorm, *, bsz, seq, tb):
    n = pm.shape[0]
    nt = seq // tb
    nqk = G_HEADS * G_DK
    nv = G_HEADS * G_DV
    rowmap = lambda b, t: (b * nt + t, 0)
    return pl.pallas_call(
        functools.partial(_gla_kernel, tb=tb),
        grid=(bsz, nt),
        in_specs=[
            pl.BlockSpec((tb, nqk), lambda b, t: (b * nt + t, 6)),
            pl.BlockSpec((tb, nqk), lambda b, t: (b * nt + t, 7)),
            pl.BlockSpec((tb, nv), lambda b, t: (b * nt + t, 4)),
            pl.BlockSpec((tb, nv), lambda b, t: (b * nt + t, 5)),
            pl.BlockSpec((tb, GATE_LANES), rowmap),
            pl.BlockSpec((GATE_LANES, nqk), lambda b, t: (0, 0)),
            pl.BlockSpec((1, nqk), lambda b, t: (0, 0)),
            pl.BlockSpec((1, nv), lambda b, t: (0, 0)),
        ],
        out_specs=pl.BlockSpec((tb, nv), rowmap),
        out_shape=jax.ShapeDtypeStruct((n, nv), BF16),
        scratch_shapes=[pltpu.VMEM((G_HEADS, G_DV, G_DK), F32)],
        compiler_params=_cparams("parallel", "arbitrary"),
        name="gla",
    )(pm, pm, pm, pm, gates, wg_pad, bg, hnorm)


def _outproj_kernel(hm_ref, hg_ref, w1_ref, w2_ref, h_ref, o_ref):
    o_ref[...] = h_ref[...] + _dot(hm_ref[...], w1_ref[...]) + _dot(hg_ref[...], w2_ref[...])


def _outproj(hm, hg, w_out, e, h, *, tm, tn):
    n, d = h.shape
    kh = hm.shape[1]
    return pl.pallas_call(
        _outproj_kernel,
        grid=(n // tm, d // tn),
        in_specs=[
            pl.BlockSpec((tm, kh), lambda i, j: (i, 0)),
            pl.BlockSpec((tm, kh), lambda i, j: (i, 0)),
            pl.BlockSpec((None, kh, tn), lambda i, j: (e, 0, j)),
            pl.BlockSpec((None, kh, tn), lambda i, j: (e, 1, j)),
            pl.BlockSpec((tm, tn), lambda i, j: (i, j)),
        ],
        out_specs=pl.BlockSpec((tm, tn), lambda i, j: (i, j)),
        out_shape=jax.ShapeDtypeStruct((n, d), F32),
        compiler_params=_cparams("parallel", "arbitrary"),
        name="outproj",
    )(hm, hg, w_out, w_out, h)


FFN_HALO = 16
MXU_COLS = 256
NORM_ROWS = 128
GLU_ROWS = 64


def _rms_rows_to(dst_ref, dst_off, src_ref, g_ref, rows):
    g = g_ref[...]
    for r0 in range(0, rows, NORM_ROWS):
        nr = min(NORM_ROWS, rows - r0)
        dst_ref[dst_off + r0:dst_off + r0 + nr, :] = _rms(src_ref[r0:r0 + nr, :], g).astype(dst_ref.dtype)


def _ffn_kernel(h_ref, halo_ref, g_ref, wg_ref, wv_ref, cwg_ref, cwv_ref, cbg_ref, cbv_ref, wdn_ref,
                gpost_ref, *rest, tm, tk, rb, tiles_per_seq, post):
    if post == "extra":
        o_ref, u_ref, xn_ref, a_ref = rest
    else:
        o_ref, xn_ref, a_ref = rest
    i = pl.program_id(0)

    @pl.when(pl.program_id(1) == 0)
    def _():
        _rms_rows_to(xn_ref, 0, halo_ref, g_ref, FFN_HALO)
        _rms_rows_to(xn_ref, FFN_HALO, h_ref, g_ref, tm)
        o_ref[...] = h_ref[...]

    keep = ((i % tiles_per_seq) != 0).astype(F32)

    def up(r0, nr):
        for c0 in range(0, tk, MXU_COLS):
            a_ref[r0:r0 + nr, c0:c0 + MXU_COLS] = _dot(xn_ref[r0:r0 + nr, :], wg_ref[:, c0:c0 + MXU_COLS])
            a_ref[r0:r0 + nr, tk + c0:tk + c0 + MXU_COLS] = _dot(xn_ref[r0:r0 + nr, :], wv_ref[:, c0:c0 + MXU_COLS])

    up(0, FFN_HALO + rb)
    a_ref[0:FFN_HALO, :] = a_ref[0:FFN_HALO, :] * keep
    for r in range(1, tm // rb):
        up(FFN_HALO + r * rb, rb)
    w0 = jnp.concatenate([cwg_ref[0:1, :], cwv_ref[0:1, :]], axis=-1)
    w1 = jnp.concatenate([cwg_ref[1:2, :], cwv_ref[1:2, :]], axis=-1)
    w2 = jnp.concatenate([cwg_ref[2:3, :], cwv_ref[2:3, :]], axis=-1)
    cb = jnp.concatenate([cbg_ref[...], cbv_ref[...]], axis=-1)
    for r in range(tm // rb):
        r0 = FFN_HALO + r * rb
        c = (w2 * a_ref[r0:r0 + rb, :] + w1 * a_ref[r0 - 1:r0 - 1 + rb, :]
             + w0 * a_ref[r0 - 2:r0 - 2 + rb, :] + cb)
        hg = 0.5 * c[:, 0:tk]
        act = ((hg + hg * jnp.tanh(hg)) * c[:, tk:2 * tk]).astype(BF16)
        o_ref[r * rb:(r + 1) * rb, :] += _dot(act, wdn_ref[...])

    if post is not None:
        @pl.when(pl.program_id(1) == pl.num_programs(1) - 1)
        def _():
            _rms_rows_to(u_ref if post == "extra" else o_ref, 0, o_ref, gpost_ref, tm)


def _ffn(h, g, wup, cw, cb, wdn, layer, gpost, post, *, seq, tm, tk, rb):
    n, d = h.shape
    nk = wdn.shape[1] // tk
    hb = tm // FFN_HALO
    out_specs = [pl.BlockSpec((tm, d), lambda i, k: (i, 0))]
    out_shape = [jax.ShapeDtypeStruct((n, d), F32)]
    if post == "extra":
        out_specs.append(pl.BlockSpec((tm, d), lambda i, k: (i, 0)))
        out_shape.append(jax.ShapeDtypeStruct((n, d), BF16))
    return pl.pallas_call(
        functools.partial(_ffn_kernel, tm=tm, tk=tk, rb=rb, tiles_per_seq=seq // tm, post=post),
        grid=(n // tm, nk),
        in_specs=[
            pl.BlockSpec((tm, d), lambda i, k: (i, 0)),
            pl.BlockSpec((FFN_HALO, d), lambda i, k: (jnp.maximum(i * hb - 1, 0), 0)),
            pl.BlockSpec((1, d), lambda i, k: (0, 0)),
            pl.BlockSpec((None, None, d, tk), lambda i, k: (layer, k, 0, 0)),
            pl.BlockSpec((None, None, d, tk), lambda i, k: (layer, k, 0, 1)),
            pl.BlockSpec((None, FFN_CONV, tk), lambda i, k: (layer, 0, k)),
            pl.BlockSpec((None, FFN_CONV, tk), lambda i, k: (layer, 0, nk + k)),
            pl.BlockSpec((None, 1, tk), lambda i, k: (layer, 0, k)),
            pl.BlockSpec((None, 1, tk), lambda i, k: (layer, 0, nk + k)),
            pl.BlockSpec((None, tk, d), lambda i, k: (layer, k, 0)),
            pl.BlockSpec((1, d), lambda i, k: (0, 0)),
        ],
        out_specs=out_specs,
        out_shape=out_shape,
        scratch_shapes=[
            pltpu.VMEM((tm + FFN_HALO, d), BF16),
            pltpu.VMEM((tm + FFN_HALO, 2 * tk), F32),
        ],
        compiler_params=_cparams("parallel", "arbitrary"),
        name="convffn",
    )(h, h, g, wup, wup, cw, cw, cb, cb, wdn, gpost)


S5_TILE = SUBCHUNK * S5_P
S5_W = CHUNK * S5_P


def _s5_gen_kernel(lr_ref, li_ref, lrc_ref, lic_ref, dt_ref, bre_ref, bim_ref, cre_ref, cim_ref,
                   strip_ref, wzr_ref, wzi_ref, ptr_ref, pti_ref, ar_ref, ai_ref):
    dt = jnp.exp(dt_ref[0])
    lam_r = lr_ref[0]
    lam_i = li_ref[0]
    xr = lam_r * dt
    xi = lam_i * dt
    er = jnp.exp(xr)
    lbr = er * jnp.cos(xi)
    lbi = er * jnp.sin(xi)
    den = lam_r * lam_r + lam_i * lam_i
    cfr = ((lbr - 1.0) * lam_r + lbi * lam_i) / den
    cfi = (lbi * lam_r - (lbr - 1.0) * lam_i) / den
    bbr = cfr * bre_ref[0] - cfi * bim_ref[0]
    bbi = cfr * bim_ref[0] + cfi * bre_ref[0]
    kk = lax.broadcasted_iota(jnp.int32, (CHUNK, S5_N), 0).astype(F32)
    pe = jnp.exp(kk * xr)
    pwr = pe * jnp.cos(kk * xi)
    pwi = pe * jnp.sin(kk * xi)
    for s in range(CHUNK):
        pr = pwr[CHUNK - 1 - s:CHUNK - s, :]
        pi = pwi[CHUNK - 1 - s:CHUNK - s, :]
        wzr_ref[0, S5_P * s:S5_P * (s + 1), :] = (bbr * pr - bbi * pi).astype(wzr_ref.dtype)
        wzi_ref[0, S5_P * s:S5_P * (s + 1), :] = (bbr * pi + bbi * pr).astype(wzi_ref.dtype)
    e64 = jnp.exp(CHUNK * xr)
    ar_ref[0] = e64 * jnp.cos(CHUNK * xi)
    ai_ref[0] = e64 * jnp.sin(CHUNK * xi)
    xrc = lrc_ref[0] * dt
    xic = lic_ref[0] * dt
    tt = lax.broadcasted_iota(jnp.int32, (S5_N, CHUNK), 1).astype(F32)
    pte = jnp.exp(tt * xrc)
    ptr = pte * jnp.cos(tt * xic)
    pti = pte * jnp.sin(tt * xic)
    lane = lax.broadcasted_iota(jnp.int32, (CHUNK, S5_W), 1)
    rep_t = ((lane // S5_P) == lax.broadcasted_iota(jnp.int32, (CHUNK, S5_W), 0)).astype(F32)
    lane_p = lax.broadcasted_iota(jnp.int32, (S5_P, S5_W), 1)
    rep_p = ((lane_p % S5_P) == lax.broadcasted_iota(jnp.int32, (S5_P, S5_W), 0)).astype(F32)
    pr_rep = _dot(ptr, rep_t, precision=HI)
    pi_rep = _dot(pti, rep_t, precision=HI)
    cr_rep = _dot(cre_ref[0], rep_p, precision=HI)
    ci_rep = _dot(cim_ref[0], rep_p, precision=HI)
    q0r = cr_rep * pr_rep - ci_rep * pi_rep
    q0i = cr_rep * pi_rep + ci_rep * pr_rep
    erc = jnp.exp(xrc)
    lbrc = erc * jnp.cos(xic)
    lbic = erc * jnp.sin(xic)
    ptr_ref[0] = (q0r * lbrc - q0i * lbic).astype(ptr_ref.dtype)
    pti_ref[0] = (-(q0r * lbic + q0i * lbrc)).astype(pti_ref.dtype)
    kern = _dot(bbr, q0r, precision=HI) - _dot(bbi, q0i, precision=HI)
    lane_w = lax.broadcasted_iota(jnp.int32, (S5_P, S5_W), 1)
    for s in range(SUBCHUNK):
        blk = kern if s == 0 else jnp.where(lane_w >= S5_P * s, pltpu.roll(kern, S5_P * s, axis=1), 0.0)
        strip_ref[0, S5_P * s:S5_P * (s + 1), :] = blk.astype(strip_ref.dtype)


def _s5_gen(lam_re, lam_im, log_dt, b_re_t, b_im_t, c_re_t, c_im_t):
    g = lam_re.shape[0]
    row3 = lambda a: a.reshape(g, 1, -1)
    col3 = lambda a: a.reshape(g, -1, 1)
    blk = lambda s: pl.BlockSpec((1,) + s, lambda i: (i, 0, 0))
    return pl.pallas_call(
        _s5_gen_kernel,
        grid=(g,),
        in_specs=[blk((1, S5_N)), blk((1, S5_N)), blk((S5_N, 1)), blk((S5_N, 1)), blk((1, 1)),
                  blk((S5_P, S5_N)), blk((S5_P, S5_N)), blk((S5_N, S5_P)), blk((S5_N, S5_P))],
        out_specs=[blk((S5_TILE, S5_W)), blk((S5_W, S5_N)), blk((S5_W, S5_N)),
                   blk((S5_N, S5_W)), blk((S5_N, S5_W)), blk((1, S5_N)), blk((1, S5_N))],
        out_shape=[
            jax.ShapeDtypeStruct((g, S5_TILE, S5_W), BF16),
            jax.ShapeDtypeStruct((g, S5_W, S5_N), BF16),
            jax.ShapeDtypeStruct((g, S5_W, S5_N), BF16),
            jax.ShapeDtypeStruct((g, S5_N, S5_W), BF16),
            jax.ShapeDtypeStruct((g, S5_N, S5_W), BF16),
            jax.ShapeDtypeStruct((g, 1, S5_N), F32),
            jax.ShapeDtypeStruct((g, 1, S5_N), F32),
        ],
        compiler_params=_cparams("parallel"),
        name="s5_gen",
    )(row3(lam_re), row3(lam_im), col3(lam_re), col3(lam_im), log_dt.reshape(g, 1, 1),
      b_re_t, b_im_t, c_re_t, c_im_t)


def _s5_apply_kernel(u_ref, strip_ref, wzr_ref, wzi_ref, ptr_ref, pti_ref, ar_ref, ai_ref,
                     y_ref, zr_scr, zi_scr, xr_scr, xi_scr, *, bsz, nchunks):
    u = u_ref[0]
    zr_scr[...] = _dot(u, wzr_ref[0])
    zi_scr[...] = _dot(u, wzi_ref[0])
    a_r = ar_ref[0]
    a_i = ai_ref[0]

    def step(c, carry):
        x_r, x_i = carry
        rows = pl.ds(pl.multiple_of(c * bsz, bsz), bsz)
        xr_scr[rows, :] = x_r
        xi_scr[rows, :] = x_i
        n_r = a_r * x_r - a_i * x_i + zr_scr[rows, :]
        n_i = a_r * x_i + a_i * x_r + zi_scr[rows, :]
        return n_r, n_i

    zero = jnp.zeros((bsz, S5_N), F32)
    lax.fori_loop(0, nchunks, step, (zero, zero))
    xr = xr_scr[...].astype(BF16)
    xi = xi_scr[...].astype(BF16)
    nt = S5_W // S5_TILE
    for j in range(nt):
        cols = slice(S5_TILE * j, S5_TILE * (j + 1))
        acc = _dot(xr, ptr_ref[0, :, cols]) + _dot(xi, pti_ref[0, :, cols])
        for i in range(j + 1):
            acc = acc + _dot(u[:, S5_TILE * i:S5_TILE * (i + 1)],
                             strip_ref[0, :, S5_TILE * (j - i):S5_TILE * (j - i + 1)])
        y_ref[0, :, cols] = acc.astype(y_ref.dtype)


def _s5_apply(ut, strip, wzr, wzi, ptr, pti, ar, ai, *, bsz, nchunks):
    g, rows, _ = ut.shape
    blk = lambda s: pl.BlockSpec((1,) + s, lambda i: (i, 0, 0))
    return pl.pallas_call(
        functools.partial(_s5_apply_kernel, bsz=bsz, nchunks=nchunks),
        grid=(g,),
        in_specs=[blk((rows, S5_W)), blk((S5_TILE, S5_W)), blk((S5_W, S5_N)), blk((S5_W, S5_N)),
                  blk((S5_N, S5_W)), blk((S5_N, S5_W)), blk((1, S5_N)), blk((1, S5_N))],
        out_specs=blk((rows, S5_W)),
        out_shape=jax.ShapeDtypeStruct((g, rows, S5_W), BF16),
        scratch_shapes=[pltpu.VMEM((rows, S5_N), F32)] * 4,
        compiler_params=_cparams("parallel"),
        name="s5_apply",
    )(ut, strip, wzr, wzi, ptr, pti, ar, ai)


def _s5_glu_kernel(h_ref, y_ref, g_ref, d_ref, w_ref, b_ref, o_ref, yv_ref, yb_ref, *, tn):
    j = pl.program_id(1)

    @pl.when(j == 0)
    def _():
        g = g_ref[...]
        dv = d_ref[...]

        def body(b, carry):
            rs = pl.ds(pl.multiple_of(b * GLU_ROWS, GLU_ROWS), GLU_ROWS)
            y = y_ref[rs, :].astype(F32) + dv * _rms(h_ref[rs, :], g)
            y = 0.5 * y * (1.0 + jnp.tanh(math.sqrt(2.0 / math.pi) * (y + 0.044715 * (y * y * y))))
            yv_ref[rs, :] = y
            yb_ref[rs, :] = y.astype(BF16)
            return carry

        lax.fori_loop(0, h_ref.shape[0] // GLU_ROWS, body, 0)

    cols = pl.ds(pl.multiple_of(j * tn, tn), tn)
    z = _dot(yb_ref[...], w_ref[...]) + b_ref[...]
    o_ref[...] = h_ref[:, cols] + yv_ref[:, cols] * _sigmoid(z)


def _s5_glu(h, y, g, dvec, w_glu, o, b_glu, *, tm, tn):
    n, d = h.shape
    return pl.pallas_call(
        functools.partial(_s5_glu_kernel, tn=tn),
        grid=(n // tm, d // tn),
        in_specs=[
            pl.BlockSpec((tm, d), lambda i, j: (i, 0)),
            pl.BlockSpec((tm, d), lambda i, j: (i, 0)),
            pl.BlockSpec((1, d), lambda i, j: (0, 0)),
            pl.BlockSpec((1, d), lambda i, j: (0, 0)),
            pl.BlockSpec((None, d, tn), lambda i, j: (o, 0, j)),
            pl.BlockSpec((1, tn), lambda i, j: (0, j)),
        ],
        out_specs=pl.BlockSpec((tm, tn), lambda i, j: (i, j)),
        out_shape=jax.ShapeDtypeStruct((n, d), F32),
        scratch_shapes=[pltpu.VMEM((tm, d), F32), pltpu.VMEM((tm, d), BF16)],
        compiler_params=_cparams("parallel", "arbitrary"),
        name="s5_glu",
    )(h, y, g, dvec, w_glu, b_glu)


def _even_layer(h, g_mix, w_main, w_gate, e, m_conv_w, m_conv_b, m_b_igate, m_b_fgate, m_head_norm,
                g_w_gate, g_b_gate, g_head_norm, w_out, *, bsz, seq, tm, tb):
    n, d = h.shape
    pm, gates = _inproj(h, g_mix.reshape(1, d), w_main, w_gate, e, tm=tm, tn=1024)
    gates_t = gates[:, 0:2 * M_HEADS].reshape(bsz, seq // CHUNK, CHUNK, 2, M_HEADS).transpose(3, 0, 1, 4, 2)
    gates_t = gates_t.reshape(2, bsz, seq // CHUNK * (M_HEADS // 2), 2 * CHUNK)
    bias = jnp.repeat(jnp.stack([m_b_igate, m_b_fgate]).astype(F32), CHUNK, axis=-1)
    bias = bias.reshape(2, M_HEADS // 2, 2 * CHUNK)
    hm = _mlstm(pm, gates_t, m_conv_w, m_conv_b.reshape(1, -1), bias, m_head_norm.reshape(1, -1),
                bsz=bsz, seq=seq, tb=tb)
    wg_pad = jnp.zeros((GATE_LANES, G_HEADS * G_DK), BF16).at[16:16 + G_RANK].set(g_w_gate.astype(BF16))
    hg = _gla(pm, gates, wg_pad, g_b_gate.reshape(1, -1), g_head_norm.reshape(1, -1),
              bsz=bsz, seq=seq, tb=tb)
    return _outproj(hm, hg, w_out, e, h, tm=tm, tn=1024)


def _odd_layer(h, u, g_mix, lam_re, lam_im, log_dt, b_re, b_im, c_re, c_im, dvec, w_glu, o, b_glu,
               *, bsz, seq, tm):
    n, d = h.shape
    groups = d // S5_P
    nchunks = seq // CHUNK
    ops = _s5_gen(lam_re, lam_im, log_dt, jnp.swapaxes(b_re, 1, 2), jnp.swapaxes(b_im, 1, 2),
                  jnp.swapaxes(c_re, 1, 2), jnp.swapaxes(c_im, 1, 2))
    ut = u.reshape(bsz, nchunks, CHUNK, groups, S5_P).transpose(3, 1, 0, 2, 4)
    ut = ut.reshape(groups, nchunks * bsz, S5_W)
    yt = _s5_apply(ut, *ops, bsz=bsz, nchunks=nchunks)
    y = yt.reshape(groups, nchunks, bsz, CHUNK, S5_P).transpose(2, 1, 3, 0, 4).reshape(n, d)
    return _s5_glu(h, y, g_mix.reshape(1, d), dvec.reshape(1, d), w_glu, o, b_glu.reshape(1, d),
                   tm=tm, tn=1024)


def _regroup_kernel(w_ref, o_ref, *, segments):
    for src, dst, n in segments:
        if src is None:
            o_ref[:, dst:dst + n] = jnp.zeros((o_ref.shape[0], n), o_ref.dtype)
        else:
            o_ref[:, dst:dst + n] = w_ref[:, src:src + n].astype(o_ref.dtype)


def _regroup_cols(w, segments, width, dtype, *, rows):
    nl, r, c = w.shape
    return pl.pallas_call(
        functools.partial(_regroup_kernel, segments=segments),
        grid=(nl, r // rows),
        in_specs=[pl.BlockSpec((None, rows, c), lambda l, i: (l, i, 0))],
        out_specs=pl.BlockSpec((None, rows, width), lambda l, i: (l, i, 0)),
        out_shape=jax.ShapeDtypeStruct((nl, r, width), dtype),
        compiler_params=_cparams("parallel", "parallel"),
        name="regroup_cols",
    )(w)


def _tile_up_kernel(w_ref, o_ref, *, dff, tk):
    for k in range(o_ref.shape[0]):
        n = min(tk, dff - k * tk)
        for half, base in ((0, 0), (1, dff)):
            o_ref[k, :, half * tk:half * tk + n] = w_ref[:, base + k * tk:base + k * tk + n].astype(o_ref.dtype)
            if n < tk:
                o_ref[k, :, half * tk + n:(half + 1) * tk] = jnp.zeros((o_ref.shape[1], tk - n), o_ref.dtype)


def _tile_up(w, tk, *, rows):
    nl, d, two_ff = w.shape
    dff = two_ff // 2
    nk = -(-dff // tk)
    return pl.pallas_call(
        functools.partial(_tile_up_kernel, dff=dff, tk=tk),
        grid=(nl, d // rows),
        in_specs=[pl.BlockSpec((None, rows, two_ff), lambda l, i: (l, i, 0))],
        out_specs=pl.BlockSpec((None, nk, rows, 2 * tk), lambda l, i: (l, 0, i, 0)),
        out_shape=jax.ShapeDtypeStruct((nl, nk, d, 2 * tk), BF16),
        compiler_params=_cparams("parallel", "parallel"),
        name="tile_up",
    )(w)


def _prep_ffn(ffn_w_up, ffn_conv_w, ffn_conv_b, ffn_w_down, tk):
    dff = ffn_w_down.shape[1]
    pad = -dff % tk
    dffp = dff + pad

    def padded(a, dtype):
        lead = a.shape[:-1]
        gv = jnp.pad(a.reshape(lead + (2, dff)), [(0, 0)] * len(lead) + [(0, 0), (0, pad)])
        return gv.astype(dtype).reshape(lead + (2 * dffp,))

    wup = _tile_up(ffn_w_up, tk, rows=128)
    cw = padded(ffn_conv_w, F32)
    cb = padded(ffn_conv_b, F32)[:, None, :]
    wdn = jnp.pad(ffn_w_down, [(0, 0), (0, pad), (0, 0)]).astype(BF16)
    return wup, cw, cb, wdn


def kernel(x, norm_mix, norm_ffn, ffn_w_up, ffn_conv_w, ffn_conv_b, ffn_w_down, norm_final,
           w_in, m_conv_w, m_conv_b, m_b_igate, m_b_fgate, m_head_norm,
           g_w_gate, g_b_gate, g_head_norm, w_out,
           s5_lambda_re, s5_lambda_im, s5_log_dt, s5_b_re, s5_b_im, s5_c_re, s5_c_im,
           s5_d, s5_w_glu, s5_b_glu):
    return _forward(x, norm_mix, norm_ffn, ffn_w_up, ffn_conv_w, ffn_conv_b, ffn_w_down, norm_final,
                    w_in, m_conv_w, m_conv_b, m_b_igate, m_b_fgate, m_head_norm,
                    g_w_gate, g_b_gate, g_head_norm, w_out,
                    s5_lambda_re, s5_lambda_im, s5_log_dt, s5_b_re, s5_b_im, s5_c_re, s5_c_im,
                    s5_d, s5_w_glu, s5_b_glu, tm=512, tf=512, tb=512)


def _forward(x, norm_mix, norm_ffn, ffn_w_up, ffn_conv_w, ffn_conv_b, ffn_w_down, norm_final,
             w_in, m_conv_w, m_conv_b, m_b_igate, m_b_fgate, m_head_norm,
             g_w_gate, g_b_gate, g_head_norm, w_out,
             s5_lambda_re, s5_lambda_im, s5_log_dt, s5_b_re, s5_b_im, s5_c_re, s5_c_im,
             s5_d, s5_w_glu, s5_b_glu, *, tm, tf, tb):
    bsz, seq, d = x.shape
    depth = norm_mix.shape[0]
    n = bsz * seq
    tmd = min(2 * tm, n)
    h = x.reshape(n, d)

    wup, cw, cb, wdn = _prep_ffn(ffn_w_up, ffn_conv_w, ffn_conv_b, ffn_w_down, 512)
    c0 = 2 * M_HEADS * M_DK + 2 * M_HEADS * M_DV
    c1 = c0 + 2 * M_HEADS
    c2 = c1 + 2 * G_HEADS * G_DK + 2 * G_HEADS * G_DV
    ng = (c1 - c0) + G_RANK
    w_main = _regroup_cols(w_in, ((0, 0, c0), (c1, c0, c2 - c1)), c0 + c2 - c1, BF16, rows=256)
    w_gate = _regroup_cols(w_in, ((c0, 0, c1 - c0), (c2, c1 - c0, G_RANK), (None, ng, GATE_LANES - ng)),
                           GATE_LANES, BF16, rows=256)
    w_out_b = w_out.astype(BF16)
    w_glu_b = s5_w_glu.astype(BF16)

    for layer in range(depth):
        if layer % 2 == 0:
            e = layer // 2
            h = _even_layer(h, norm_mix[layer], w_main, w_gate, e, m_conv_w[e], m_conv_b[e],
                            m_b_igate[e], m_b_fgate[e], m_head_norm[e], g_w_gate[e], g_b_gate[e],
                            g_head_norm[e], w_out_b, bsz=bsz, seq=seq, tm=tmd, tb=tb)
        else:
            o = layer // 2
            h = _odd_layer(h, u, norm_mix[layer], s5_lambda_re[o], s5_lambda_im[o], s5_log_dt[o],
                           s5_b_re[o], s5_b_im[o], s5_c_re[o], s5_c_im[o], s5_d[o], w_glu_b, o,
                           s5_b_glu[o], bsz=bsz, seq=seq, tm=tm)
        if layer == depth - 1:
            post, gpost = "inplace", norm_final
        elif layer % 2 == 0:
            post, gpost = "extra", norm_mix[layer + 1]
        else:
            post, gpost = None, norm_ffn[layer]
        outs = _ffn(h, norm_ffn[layer].reshape(1, d), wup, cw, cb, wdn, layer, gpost.reshape(1, d), post,
                    seq=seq, tm=tf, tk=512, rb=min(256, tf))
        h = outs[0]
        u = outs[1] if post == "extra" else None
    return h.reshape(bsz, seq, d)
```

```python
import functools
import math

import jax
import jax.numpy as jnp
from jax import lax
from jax.experimental import pallas as pl
from jax.experimental.pallas import tpu as pltpu

F32 = jnp.float32
BF16 = jnp.bfloat16
HI = lax.Precision.HIGHEST

EPS = 1e-6
CHUNK = 64
M_HEADS, M_DK, M_DV, M_CONV = 8, 64, 128, 4
G_HEADS, G_DK, G_DV, G_RANK, G_TAU = 4, 128, 256, 16, 16.0
S5_P, S5_N = 16, 64
FFN_CONV = 3
GATE_LANES = 128
SUBCHUNK = 16

VMEM_LIMIT = 56 * 1024 * 1024


def _cparams(*sem):
    return pltpu.CompilerParams(dimension_semantics=sem, vmem_limit_bytes=VMEM_LIMIT)


def _rms(x, g):
    return x * lax.rsqrt(jnp.mean(x * x, axis=-1, keepdims=True) + EPS) * g


def _sigmoid(x):
    return 1.0 / (1.0 + jnp.exp(-x))


def _log_sigmoid(x):
    return jnp.minimum(x, 0.0) - jnp.log(1.0 + jnp.exp(-jnp.abs(x)))


def _dot(a, b, **kw):
    return jnp.dot(a, b, preferred_element_type=F32, **kw)


def _dot_nt(a, b, **kw):
    return lax.dot_general(a, b, (((1,), (1,)), ((), ())), preferred_element_type=F32, **kw)


def _dot_tn(a, b, **kw):
    return lax.dot_general(a, b, (((0,), (0,)), ((), ())), preferred_element_type=F32, **kw)


def _inproj_kernel(x_ref, g_ref, w_ref, wg_ref, o_ref, og_ref, xn_ref):
    @pl.when(pl.program_id(1) == 0)
    def _():
        _rms_rows_to(xn_ref, 0, x_ref, g_ref, x_ref.shape[0])
        og_ref[...] = _dot(xn_ref[...], wg_ref[...])

    o_ref[...] = _dot(xn_ref[...], w_ref[...]).astype(o_ref.dtype)


def _inproj(h, g, w_main, w_gate, e, *, tm, tn):
    n, d = h.shape
    wn = w_main.shape[2]
    return pl.pallas_call(
        _inproj_kernel,
        grid=(n // tm, wn // tn),
        in_specs=[
            pl.BlockSpec((tm, d), lambda i, j: (i, 0)),
            pl.BlockSpec((1, d), lambda i, j: (0, 0)),
            pl.BlockSpec((None, d, tn), lambda i, j: (e, 0, j)),
            pl.BlockSpec((None, d, GATE_LANES), lambda i, j: (e, 0, 0)),
        ],
        out_specs=[
            pl.BlockSpec((tm, tn), lambda i, j: (i, j)),
            pl.BlockSpec((tm, GATE_LANES), lambda i, j: (i, 0)),
        ],
        out_shape=[
            jax.ShapeDtypeStruct((n, wn), BF16),
            jax.ShapeDtypeStruct((n, GATE_LANES), F32),
        ],
        scratch_shapes=[pltpu.VMEM((tm, d), BF16)],
        compiler_params=_cparams("parallel", "arbitrary"),
        name="inproj",
    )(h, g, w_main, w_gate)


def _mlstm_select():
    npair = M_HEADS // 2
    npc = 2 * npair * 2 * CHUNK
    jrow = lax.broadcasted_iota(jnp.int32, (96, npc + M_HEADS * M_DV), 0)
    ncol = lax.broadcasted_iota(jnp.int32, (96, npc + M_HEADS * M_DV), 1)
    half = jrow // 48
    j16 = jrow % 16
    sel_pair = (ncol < npc) & (ncol // (2 * CHUNK) == j16) & ((ncol % (2 * CHUNK)) // CHUNK == half) & (j16 < 8)
    sel_em = (ncol >= npc) & (j16 >= 8) & (j16 < 12) & ((ncol - npc) // M_DV == 2 * (j16 - 8) + half)
    return (sel_pair | sel_em).astype(F32)


def _mlstm_kernel(mq_ref, mk_ref, mv_ref, mo_ref, li_ref, lf_ref, cw_ref, cb_ref, bi_ref, bf_ref, hn_ref,
                  sel_ref, o_ref, qk_scr, qkc_scr, c_scr, m_scr, row_scr, rep_scr, *, tb):
    @pl.when(pl.program_id(1) == 0)
    def _():
        qk_scr[0:8, :] = jnp.zeros((8, 2 * M_HEADS * M_DK), F32)
        c_scr[...] = jnp.zeros(c_scr.shape, F32)
        m_scr[...] = jnp.zeros(m_scr.shape, F32)

    nqk = M_HEADS * M_DK
    qk_scr[8:8 + tb, 0:nqk] = mq_ref[...].astype(F32)
    qk_scr[8:8 + tb, nqk:2 * nqk] = mk_ref[...].astype(F32)
    conv = cb_ref[...] + cw_ref[0:1, :] * qk_scr[5:5 + tb, :]
    for j in range(1, M_CONV):
        conv = conv + cw_ref[j:j + 1, :] * qk_scr[5 + j:5 + j + tb, :]
    tail = qk_scr[tb:tb + 8, :]
    qkc_scr[...] = conv * _sigmoid(conv)
    qk_scr[0:8, :] = tail

    npair = M_HEADS // 2
    lane = lax.broadcasted_iota(jnp.int32, (1, 2 * CHUNK), 1)
    lo_half = lane < CHUNK
    pos = lane % CHUNK
    trow = lax.broadcasted_iota(jnp.int32, (CHUNK, 2 * CHUNK), 0)
    causal2 = (lax.broadcasted_iota(jnp.int32, (CHUNK, 2 * CHUNK), 1) % CHUNK) <= trow
    r2 = lax.broadcasted_iota(jnp.int32, (2 * CHUNK, 2 * CHUNK), 0)
    c2 = lax.broadcasted_iota(jnp.int32, (2 * CHUNK, 2 * CHUNK), 1)
    tri2 = ((r2 // CHUNK == c2 // CHUNK) & (r2 <= c2)).astype(F32)
    row_dk = lax.broadcasted_iota(jnp.int32, (2 * M_DK, 1), 0)
    row_lo = row_dk < M_DK
    ones_v = jnp.ones((CHUNK, M_DV), BF16)
    zeros_v = jnp.zeros((CHUNK, 2 * M_DV), BF16)
    ones_sum = jnp.ones((M_DV, M_DV), BF16)
    ones_n = jnp.ones((CHUNK, M_DV), F32)
    scale = M_DK ** -0.5
    sel = sel_ref[...]

    def exact3(x):
        hi = x.astype(BF16).astype(F32)
        mid = (x - hi).astype(BF16).astype(F32)
        return hi, mid, x - hi - mid

    def half_max(x):
        m0 = jnp.max(jnp.where(lo_half, x, -jnp.inf), axis=-1, keepdims=True)
        m1 = jnp.max(jnp.where(lo_half, -jnp.inf, x), axis=-1, keepdims=True)
        return jnp.where(lo_half, m0, m1)

    nchunk = tb // CHUNK
    li = li_ref[0] + bi_ref[...]
    b = _dot(_log_sigmoid(lf_ref[0] + bf_ref[...]), tri2, precision=HI)
    a = li - b
    cm = a
    for sh in (1, 2, 4, 8, 16, 32):
        cm = jnp.maximum(cm, jnp.where(pos >= sh, pltpu.roll(cm, sh, 1), -jnp.inf))
    b_last = jnp.where(lo_half, b[:, CHUNK - 1:CHUNK], b[:, 2 * CHUNK - 1:2 * CHUNK])
    g = b_last + a
    gmax = half_max(g)
    m = m_scr[...]
    m_starts = []
    for c in range(nchunk):
        m_starts.append(m)
        m = jnp.maximum(b_last[npair * c:npair * (c + 1)] + m, gmax[npair * c:npair * (c + 1)])
    m_scr[...] = m
    m_prev = jnp.concatenate(m_starts, axis=0)
    m_new = jnp.maximum(b_last + m_prev, gmax)
    m_out = b + jnp.maximum(m_prev, cm)
    e1 = b - m_out
    terms = exact3(e1) + exact3(jnp.exp(e1 + m_prev)) + exact3(jnp.exp(-m_out))
    wk = jnp.exp(g - m_new)
    decay = jnp.exp(b_last + m_prev - m_new)
    zero4 = jnp.zeros((npair, 2 * CHUNK), F32)
    for c in range(nchunk):
        rs = slice(npair * c, npair * (c + 1))
        row_scr[c, 0:npair] = a[rs]
        row_scr[c, npair:2 * npair] = wk[rs]
        row_scr[c, 2 * npair:3 * npair] = decay[rs]
        q48 = jnp.concatenate([x for t in range(3) for x in (terms[t][rs], terms[3 + t][rs], terms[6 + t][rs], zero4)],
                              axis=0)
        qfull = jnp.concatenate([q48[:, 0:CHUNK], q48[:, CHUNK:2 * CHUNK]], axis=0)
        rep_scr[c] = _dot_tn(qfull, sel)

    def chunk(c, carry):
        rows = pl.ds(pl.multiple_of(c * CHUNK, CHUNK), CHUNK)
        a = row_scr[c, 0:npair]
        wk = row_scr[c, npair:2 * npair]
        decay = row_scr[c, 2 * npair:3 * npair]
        rep = rep_scr[c]
        for p in range(npair):
            qp = qkc_scr[rows, 2 * M_DK * p:2 * M_DK * (p + 1)] * scale
            kt = qkc_scr[rows, nqk + 2 * M_DK * p:nqk + 2 * M_DK * (p + 1)].T
            kt2 = jnp.concatenate([jnp.where(row_lo, kt, 0.0), jnp.where(row_lo, 0.0, kt)], axis=1)
            e1c = rep[:, 2 * CHUNK * p:2 * CHUNK * (p + 1)]
            wic = rep[:, 2 * CHUNK * (npair + p):2 * CHUNK * (npair + p + 1)]
            dexp = jnp.exp(jnp.where(causal2, a[p:p + 1, :] + e1c, -jnp.inf))
            smat = (_dot(qp.astype(BF16), kt2.astype(BF16)) * dexp).astype(BF16)
            v0 = mv_ref[rows, M_DV * 2 * p:M_DV * (2 * p + 1)]
            v1 = mv_ref[rows, M_DV * (2 * p + 1):M_DV * (2 * p + 2)]
            vbd = jnp.concatenate([jnp.concatenate([v0, ones_v, zeros_v], axis=1),
                                   jnp.concatenate([zeros_v, v1, ones_v], axis=1)], axis=0)
            cbd = c_scr[p]
            numext = _dot(smat, vbd) + _dot((qp * wic).astype(BF16), cbd.astype(BF16))
            for e in range(2):
                h = 2 * p + e
                num = numext[:, 2 * M_DV * e:2 * M_DV * e + M_DV]
                den = numext[:, 2 * M_DV * e + M_DV:2 * M_DV * (e + 1)]
                emc = rep[:, 2 * npair * 2 * CHUNK + M_DV * h:2 * npair * 2 * CHUNK + M_DV * (h + 1)]
                hh = num / jnp.maximum(jnp.abs(den), emc)
                sq = hh * hh
                sq_hi = sq.astype(BF16)
                ms = (_dot(sq_hi, ones_sum) + _dot((sq - sq_hi.astype(F32)).astype(BF16), ones_sum)) * (1.0 / M_DV)
                y = hh * lax.rsqrt(ms + EPS) * hn_ref[:, M_DV * h:M_DV * (h + 1)]
                og = _sigmoid(mo_ref[rows, M_DV * h:M_DV * (h + 1)].astype(F32))
                o_ref[rows, M_DV * h:M_DV * (h + 1)] = (og * y).astype(o_ref.dtype)
            wkp = wk[p:p + 1, :]
            kts = jnp.where(row_lo, kt * wkp[:, 0:CHUNK], kt * wkp[:, CHUNK:2 * CHUNK])
            kts = kts.astype(BF16).astype(F32)
            uc = _dot(kts, jnp.concatenate([v0, v1], axis=1).astype(F32))
            un = _dot(kts, ones_n)
            zc = jnp.zeros((2 * M_DK, M_DV), F32)
            upd = jnp.where(row_lo, jnp.concatenate([uc[:, 0:M_DV], un, zc, zc], axis=1),
                            jnp.concatenate([zc, zc, uc[:, M_DV:2 * M_DV], un], axis=1))
            dp = decay[p:p + 1, :]
            dcol = jnp.where(row_lo, dp[:, 0:1], dp[:, CHUNK:CHUNK + 1])
            c_scr[p] = dcol * cbd + upd
        return carry

    lax.fori_loop(0, tb // CHUNK, chunk, 0, unroll=4)


def _mlstm(pm, gates_t, conv_w, conv_b, bias, hnorm, *, bsz, seq, tb):
    n = pm.shape[0]
    nt = seq // tb
    nqk = M_HEADS * M_DK
    nv = M_HEADS * M_DV
    sel = _mlstm_select()
    rpb = tb // CHUNK * (M_HEADS // 2)
    bias_i = jnp.tile(bias[0], (tb // CHUNK, 1))
    bias_f = jnp.tile(bias[1], (tb // CHUNK, 1))
    return pl.pallas_call(
        functools.partial(_mlstm_kernel, tb=tb),
        grid=(bsz, nt),
        in_specs=[
            pl.BlockSpec((tb, nqk), lambda b, t: (b * nt + t, 0)),
            pl.BlockSpec((tb, nqk), lambda b, t: (b * nt + t, 1)),
            pl.BlockSpec((tb, nv), lambda b, t: (b * nt + t, 1)),
            pl.BlockSpec((tb, nv), lambda b, t: (b * nt + t, 2)),
            pl.BlockSpec((None, 1, rpb, 2 * CHUNK), lambda b, t: (0, b, t, 0)),
            pl.BlockSpec((None, 1, rpb, 2 * CHUNK), lambda b, t: (1, b, t, 0)),
            pl.BlockSpec((M_CONV, 2 * nqk), lambda b, t: (0, 0)),
            pl.BlockSpec((1, 2 * nqk), lambda b, t: (0, 0)),
            pl.BlockSpec((rpb, 2 * CHUNK), lambda b, t: (0, 0)),
            pl.BlockSpec((rpb, 2 * CHUNK), lambda b, t: (0, 0)),
            pl.BlockSpec((1, nv), lambda b, t: (0, 0)),
            pl.BlockSpec(sel.shape, lambda b, t: (0, 0)),
        ],
        out_specs=pl.BlockSpec((tb, nv), lambda b, t: (b * nt + t, 0)),
        out_shape=jax.ShapeDtypeStruct((n, nv), BF16),
        scratch_shapes=[
            pltpu.VMEM((tb + 8, 2 * nqk), F32),
            pltpu.VMEM((tb, 2 * nqk), F32),
            pltpu.VMEM((M_HEADS // 2, 2 * M_DK, 4 * M_DV), F32),
            pltpu.VMEM((M_HEADS // 2, 2 * CHUNK), F32),
            pltpu.VMEM((tb // CHUNK, 2 * M_HEADS, 2 * CHUNK), F32),
            pltpu.VMEM((tb // CHUNK, CHUNK, sel.shape[1]), F32),
        ],
        compiler_params=_cparams("parallel", "arbitrary"),
        name="mlstm",
    )(pm, pm, pm, pm, gates_t, gates_t, conv_w, conv_b, bias_i, bias_f, hnorm, sel)


def _gla_kernel(gq_ref, gk_ref, gv_ref, gg_ref, gc_ref, wg_ref, bg_ref, hn_ref, o_ref, s_scr, bc_scr, *, tb):
    @pl.when(pl.program_id(1) == 0)
    def _():
        s_scr[...] = jnp.zeros(s_scr.shape, F32)

    row = lax.broadcasted_iota(jnp.int32, (CHUNK, CHUNK), 0)
    col = lax.broadcasted_iota(jnp.int32, (CHUNK, CHUNK), 1)
    tril = row >= col
    tri = tril.astype(F32)
    rowk = lax.broadcasted_iota(jnp.int32, (CHUNK, 1), 0)
    nsub = CHUNK // SUBCHUNK
    scale = G_DK ** -0.5

    pre = _dot(gc_ref[...].astype(BF16), wg_ref[...]) + bg_ref[...]
    la = _log_sigmoid(pre) * (1.0 / G_TAU)
    for c in range(tb // CHUNK):
        bc_scr[CHUNK * c:CHUNK * (c + 1), :] = _dot(tri, la[CHUNK * c:CHUNK * (c + 1), :], precision=HI)

    def chunk(c, carry):
        r0 = pl.multiple_of(c * CHUNK, CHUNK)
        rows = pl.ds(r0, CHUNK)
        bc_all = bc_scr[rows, :]
        for h in range(G_HEADS):
            ks = slice(G_DK * h, G_DK * (h + 1))
            vs = slice(G_DV * h, G_DV * (h + 1))
            bc = bc_all[:, ks]
            q = gq_ref[rows, ks].astype(F32) * scale
            k = gk_ref[rows, ks].astype(F32)
            v = gv_ref[rows, vs]
            st = s_scr[h]
            o = _dot_nt((q * jnp.exp(bc)).astype(BF16), st.astype(BF16))
            cblk = jnp.concatenate(
                [jnp.broadcast_to(bc[SUBCHUNK * i:SUBCHUNK * i + 1, :], (SUBCHUNK, G_DK))
                 for i in range(nsub)], axis=0)
            qt = (q * jnp.exp(bc - cblk)).astype(BF16)
            blocks = []
            for i in range(nsub):
                ci = bc[SUBCHUNK * i:SUBCHUNK * i + 1, :]
                kt = jnp.where(rowk < SUBCHUNK * (i + 1), k * jnp.exp(ci - bc), 0.0).astype(BF16)
                blocks.append(_dot_nt(qt[SUBCHUNK * i:SUBCHUNK * (i + 1), :], kt))
            a = jnp.where(tril, jnp.concatenate(blocks, axis=0), 0.0)
            o = o + _dot(a.astype(BF16), v)
            y = o * lax.rsqrt(jnp.mean(o * o, axis=-1, keepdims=True) + EPS) * hn_ref[:, vs]
            gg = gg_ref[rows, vs].astype(F32)
            o_ref[rows, vs] = (gg * _sigmoid(gg) * y).astype(o_ref.dtype)
            last = bc[CHUNK - 1:CHUNK, :]
            kd = (k * jnp.exp(last - bc)).astype(BF16)
            s_scr[h] = st * jnp.exp(last) + _dot_tn(v, kd)
        return carry

    lax.fori_loop(0, tb // CHUNK, chunk, 0, unroll=4)


def _gla(pm, gates, wg_pad, bg, hnorm, *, bsz, seq, tb):
    n = pm.shape[0]
    nt = seq // tb
    nqk = G_HEADS * G_DK
    nv = G_HEADS * G_DV
    rowmap = lambda b, t: (b * nt + t, 0)
    return pl.pallas_call(
        functools.partial(_gla_kernel, tb=tb),
        grid=(bsz, nt),
        in_specs=[
            pl.BlockSpec((tb, nqk), lambda b, t: (b * nt + t, 6)),
            pl.BlockSpec((tb, nqk), lambda b, t: (b * nt + t, 7)),
            pl.BlockSpec((tb, nv), lambda b, t: (b * nt + t, 4)),
            pl.BlockSpec((tb, nv), lambda b, t: (b * nt + t, 5)),
            pl.BlockSpec((tb, GATE_LANES), rowmap),
            pl.BlockSpec((GATE_LANES, nqk), lambda b, t: (0, 0)),
            pl.BlockSpec((1, nqk), lambda b, t: (0, 0)),
            pl.BlockSpec((1, nv), lambda b, t: (0, 0)),
        ],
        out_specs=pl.BlockSpec((tb, nv), rowmap),
        out_shape=jax.ShapeDtypeStruct((n, nv), BF16),
        scratch_shapes=[pltpu.VMEM((G_HEADS, G_DV, G_DK), F32), pltpu.VMEM((tb, nqk), F32)],
        compiler_params=_cparams("parallel", "arbitrary"),
        name="gla",
    )(pm, pm, pm, pm, gates, wg_pad, bg, hnorm)


def _outproj_kernel(hm_ref, hg_ref, w1_ref, w2_ref, h_ref, o_ref):
    o_ref[...] = h_ref[...] + _dot(hm_ref[...], w1_ref[...]) + _dot(hg_ref[...], w2_ref[...])


def _outproj(hm, hg, w_out, e, h, *, tm, tn):
    n, d = h.shape
    kh = hm.shape[1]
    return pl.pallas_call(
        _outproj_kernel,
        grid=(n // tm, d // tn),
        in_specs=[
            pl.BlockSpec((tm, kh), lambda i, j: (i, 0)),
            pl.BlockSpec((tm, kh), lambda i, j: (i, 0)),
            pl.BlockSpec((None, kh, tn), lambda i, j: (e, 0, j)),
            pl.BlockSpec((None, kh, tn), lambda i, j: (e, 1, j)),
            pl.BlockSpec((tm, tn), lambda i, j: (i, j)),
        ],
        out_specs=pl.BlockSpec((tm, tn), lambda i, j: (i, j)),
        out_shape=jax.ShapeDtypeStruct((n, d), F32),
        compiler_params=_cparams("parallel", "arbitrary"),
        name="outproj",
    )(hm, hg, w_out, w_out, h)


FFN_HALO = 16
MXU_COLS = 256
NORM_ROWS = 128
GLU_ROWS = 64


def _rms_rows_to(dst_ref, dst_off, src_ref, g_ref, rows):
    g = g_ref[...]
    for r0 in range(0, rows, NORM_ROWS):
        nr = min(NORM_ROWS, rows - r0)
        dst_ref[dst_off + r0:dst_off + r0 + nr, :] = _rms(src_ref[r0:r0 + nr, :], g).astype(dst_ref.dtype)


def _ffn_kernel(h_ref, halo_ref, g_ref, wg_ref, wv_ref, cwg_ref, cwv_ref, cbg_ref, cbv_ref, wdn_ref,
                gpost_ref, *rest, tm, tk, rb, tiles_per_seq, post):
    if post == "extra":
        o_ref, u_ref, xn_ref, a_ref = rest
    else:
        o_ref, xn_ref, a_ref = rest
    i = pl.program_id(0)

    @pl.when(pl.program_id(1) == 0)
    def _():
        _rms_rows_to(xn_ref, 0, halo_ref, g_ref, FFN_HALO)
        _rms_rows_to(xn_ref, FFN_HALO, h_ref, g_ref, tm)
        o_ref[...] = h_ref[...]

    keep = ((i % tiles_per_seq) != 0).astype(F32)

    def up(r0, nr):
        for c0 in range(0, tk, MXU_COLS):
            a_ref[r0:r0 + nr, c0:c0 + MXU_COLS] = _dot(xn_ref[r0:r0 + nr, :], wg_ref[:, c0:c0 + MXU_COLS])
            a_ref[r0:r0 + nr, tk + c0:tk + c0 + MXU_COLS] = _dot(xn_ref[r0:r0 + nr, :], wv_ref[:, c0:c0 + MXU_COLS])

    up(0, FFN_HALO + rb)
    a_ref[0:FFN_HALO, :] = a_ref[0:FFN_HALO, :] * keep
    for r in range(1, tm // rb):
        up(FFN_HALO + r * rb, rb)
    w0 = jnp.concatenate([cwg_ref[0:1, :], cwv_ref[0:1, :]], axis=-1)
    w1 = jnp.concatenate([cwg_ref[1:2, :], cwv_ref[1:2, :]], axis=-1)
    w2 = jnp.concatenate([cwg_ref[2:3, :], cwv_ref[2:3, :]], axis=-1)
    cb = jnp.concatenate([cbg_ref[...], cbv_ref[...]], axis=-1)
    for r in range(tm // rb):
        r0 = FFN_HALO + r * rb
        c = (w2 * a_ref[r0:r0 + rb, :] + w1 * a_ref[r0 - 1:r0 - 1 + rb, :]
             + w0 * a_ref[r0 - 2:r0 - 2 + rb, :] + cb)
        hg = 0.5 * c[:, 0:tk]
        act = ((hg + hg * jnp.tanh(hg)) * c[:, tk:2 * tk]).astype(BF16)
        o_ref[r * rb:(r + 1) * rb, :] += _dot(act, wdn_ref[...])

    if post is not None:
        @pl.when(pl.program_id(1) == pl.num_programs(1) - 1)
        def _():
            _rms_rows_to(u_ref if post == "extra" else o_ref, 0, o_ref, gpost_ref, tm)


def _ffn(h, g, wup, cw, cb, wdn, layer, gpost, post, *, seq, tm, tk, rb):
    n, d = h.shape
    nk = wdn.shape[1] // tk
    hb = tm // FFN_HALO
    out_specs = [pl.BlockSpec((tm, d), lambda i, k: (i, 0))]
    out_shape = [jax.ShapeDtypeStruct((n, d), F32)]
    if post == "extra":
        out_specs.append(pl.BlockSpec((tm, d), lambda i, k: (i, 0)))
        out_shape.append(jax.ShapeDtypeStruct((n, d), BF16))
    return pl.pallas_call(
        functools.partial(_ffn_kernel, tm=tm, tk=tk, rb=rb, tiles_per_seq=seq // tm, post=post),
        grid=(n // tm, nk),
        in_specs=[
            pl.BlockSpec((tm, d), lambda i, k: (i, 0)),
            pl.BlockSpec((FFN_HALO, d), lambda i, k: (jnp.maximum(i * hb - 1, 0), 0)),
            pl.BlockSpec((1, d), lambda i, k: (0, 0)),
            pl.BlockSpec((None, None, d, tk), lambda i, k: (layer, k, 0, 0)),
            pl.BlockSpec((None, None, d, tk), lambda i, k: (layer, k, 0, 1)),
            pl.BlockSpec((None, FFN_CONV, tk), lambda i, k: (layer, 0, k)),
            pl.BlockSpec((None, FFN_CONV, tk), lambda i, k: (layer, 0, nk + k)),
            pl.BlockSpec((None, 1, tk), lambda i, k: (layer, 0, k)),
            pl.BlockSpec((None, 1, tk), lambda i, k: (layer, 0, nk + k)),
            pl.BlockSpec((None, tk, d), lambda i, k: (layer, k, 0)),
            pl.BlockSpec((1, d), lambda i, k: (0, 0)),
        ],
        out_specs=out_specs,
        out_shape=out_shape,
        scratch_shapes=[
            pltpu.VMEM((tm + FFN_HALO, d), BF16),
            pltpu.VMEM((tm + FFN_HALO, 2 * tk), F32),
        ],
        compiler_params=_cparams("parallel", "arbitrary"),
        name="convffn",
    )(h, h, g, wup, wup, cw, cw, cb, cb, wdn, gpost)


S5_TILE = SUBCHUNK * S5_P
S5_W = CHUNK * S5_P


def _s5_gen_kernel(lr_ref, li_ref, lrc_ref, lic_ref, dt_ref, bre_ref, bim_ref, cre_ref, cim_ref,
                   strip_ref, wzr_ref, wzi_ref, ptr_ref, pti_ref, ar_ref, ai_ref):
    dt = jnp.exp(dt_ref[0])
    lam_r = lr_ref[0]
    lam_i = li_ref[0]
    xr = lam_r * dt
    xi = lam_i * dt
    er = jnp.exp(xr)
    lbr = er * jnp.cos(xi)
    lbi = er * jnp.sin(xi)
    den = lam_r * lam_r + lam_i * lam_i
    cfr = ((lbr - 1.0) * lam_r + lbi * lam_i) / den
    cfi = (lbi * lam_r - (lbr - 1.0) * lam_i) / den
    bbr = cfr * bre_ref[0] - cfi * bim_ref[0]
    bbi = cfr * bim_ref[0] + cfi * bre_ref[0]
    kk = lax.broadcasted_iota(jnp.int32, (CHUNK, S5_N), 0).astype(F32)
    pe = jnp.exp(kk * xr)
    pwr = pe * jnp.cos(kk * xi)
    pwi = pe * jnp.sin(kk * xi)
    for s in range(CHUNK):
        pr = pwr[CHUNK - 1 - s:CHUNK - s, :]
        pi = pwi[CHUNK - 1 - s:CHUNK - s, :]
        wzr_ref[0, S5_P * s:S5_P * (s + 1), :] = (bbr * pr - bbi * pi).astype(wzr_ref.dtype)
        wzi_ref[0, S5_P * s:S5_P * (s + 1), :] = (bbr * pi + bbi * pr).astype(wzi_ref.dtype)
    e64 = jnp.exp(CHUNK * xr)
    ar_ref[0] = e64 * jnp.cos(CHUNK * xi)
    ai_ref[0] = e64 * jnp.sin(CHUNK * xi)
    xrc = lrc_ref[0] * dt
    xic = lic_ref[0] * dt
    tt = lax.broadcasted_iota(jnp.int32, (S5_N, CHUNK), 1).astype(F32)
    pte = jnp.exp(tt * xrc)
    ptr = pte * jnp.cos(tt * xic)
    pti = pte * jnp.sin(tt * xic)
    lane = lax.broadcasted_iota(jnp.int32, (CHUNK, S5_W), 1)
    rep_t = ((lane // S5_P) == lax.broadcasted_iota(jnp.int32, (CHUNK, S5_W), 0)).astype(F32)
    lane_p = lax.broadcasted_iota(jnp.int32, (S5_P, S5_W), 1)
    rep_p = ((lane_p % S5_P) == lax.broadcasted_iota(jnp.int32, (S5_P, S5_W), 0)).astype(F32)
    pr_rep = _dot(ptr, rep_t, precision=HI)
    pi_rep = _dot(pti, rep_t, precision=HI)
    cr_rep = _dot(cre_ref[0], rep_p, precision=HI)
    ci_rep = _dot(cim_ref[0], rep_p, precision=HI)
    q0r = cr_rep * pr_rep - ci_rep * pi_rep
    q0i = cr_rep * pi_rep + ci_rep * pr_rep
    erc = jnp.exp(xrc)
    lbrc = erc * jnp.cos(xic)
    lbic = erc * jnp.sin(xic)
    ptr_ref[0] = (q0r * lbrc - q0i * lbic).astype(ptr_ref.dtype)
    pti_ref[0] = (-(q0r * lbic + q0i * lbrc)).astype(pti_ref.dtype)
    kern = _dot(bbr, q0r, precision=HI) - _dot(bbi, q0i, precision=HI)
    lane_w = lax.broadcasted_iota(jnp.int32, (S5_P, S5_W), 1)
    for s in range(SUBCHUNK):
        blk = kern if s == 0 else jnp.where(lane_w >= S5_P * s, pltpu.roll(kern, S5_P * s, axis=1), 0.0)
        strip_ref[0, S5_P * s:S5_P * (s + 1), :] = blk.astype(strip_ref.dtype)


def _s5_gen(lam_re, lam_im, log_dt, b_re_t, b_im_t, c_re_t, c_im_t):
    g = lam_re.shape[0]
    row3 = lambda a: a.reshape(g, 1, -1)
    col3 = lambda a: a.reshape(g, -1, 1)
    blk = lambda s: pl.BlockSpec((1,) + s, lambda i: (i, 0, 0))
    return pl.pallas_call(
        _s5_gen_kernel,
        grid=(g,),
        in_specs=[blk((1, S5_N)), blk((1, S5_N)), blk((S5_N, 1)), blk((S5_N, 1)), blk((1, 1)),
                  blk((S5_P, S5_N)), blk((S5_P, S5_N)), blk((S5_N, S5_P)), blk((S5_N, S5_P))],
        out_specs=[blk((S5_TILE, S5_W)), blk((S5_W, S5_N)), blk((S5_W, S5_N)),
                   blk((S5_N, S5_W)), blk((S5_N, S5_W)), blk((1, S5_N)), blk((1, S5_N))],
        out_shape=[
            jax.ShapeDtypeStruct((g, S5_TILE, S5_W), BF16),
            jax.ShapeDtypeStruct((g, S5_W, S5_N), BF16),
            jax.ShapeDtypeStruct((g, S5_W, S5_N), BF16),
            jax.ShapeDtypeStruct((g, S5_N, S5_W), BF16),
            jax.ShapeDtypeStruct((g, S5_N, S5_W), BF16),
            jax.ShapeDtypeStruct((g, 1, S5_N), F32),
            jax.ShapeDtypeStruct((g, 1, S5_N), F32),
        ],
        compiler_params=_cparams("parallel"),
        name="s5_gen",
    )(row3(lam_re), row3(lam_im), col3(lam_re), col3(lam_im), log_dt.reshape(g, 1, 1),
      b_re_t, b_im_t, c_re_t, c_im_t)


def _s5_apply_kernel(u_ref, strip_ref, wzr_ref, wzi_ref, ptr_ref, pti_ref, ar_ref, ai_ref,
                     y_ref, zr_scr, zi_scr, xr_scr, xi_scr, *, bsz, nchunks):
    u = u_ref[0]
    zr_scr[...] = _dot(u, wzr_ref[0])
    zi_scr[...] = _dot(u, wzi_ref[0])
    a_r = ar_ref[0]
    a_i = ai_ref[0]

    def step(c, carry):
        x_r, x_i = carry
        rows = pl.ds(pl.multiple_of(c * bsz, bsz), bsz)
        xr_scr[rows, :] = x_r
        xi_scr[rows, :] = x_i
        n_r = a_r * x_r - a_i * x_i + zr_scr[rows, :]
        n_i = a_r * x_i + a_i * x_r + zi_scr[rows, :]
        return n_r, n_i

    zero = jnp.zeros((bsz, S5_N), F32)
    lax.fori_loop(0, nchunks, step, (zero, zero))
    xr = xr_scr[...].astype(BF16)
    xi = xi_scr[...].astype(BF16)
    nt = S5_W // S5_TILE
    for j in range(nt):
        cols = slice(S5_TILE * j, S5_TILE * (j + 1))
        acc = _dot(xr, ptr_ref[0, :, cols]) + _dot(xi, pti_ref[0, :, cols])
        for i in range(j + 1):
            acc = acc + _dot(u[:, S5_TILE * i:S5_TILE * (i + 1)],
                             strip_ref[0, :, S5_TILE * (j - i):S5_TILE * (j - i + 1)])
        y_ref[0, :, cols] = acc.astype(y_ref.dtype)


def _s5_apply(ut, strip, wzr, wzi, ptr, pti, ar, ai, *, bsz, nchunks):
    g, rows, _ = ut.shape
    blk = lambda s: pl.BlockSpec((1,) + s, lambda i: (i, 0, 0))
    return pl.pallas_call(
        functools.partial(_s5_apply_kernel, bsz=bsz, nchunks=nchunks),
        grid=(g,),
        in_specs=[blk((rows, S5_W)), blk((S5_TILE, S5_W)), blk((S5_W, S5_N)), blk((S5_W, S5_N)),
                  blk((S5_N, S5_W)), blk((S5_N, S5_W)), blk((1, S5_N)), blk((1, S5_N))],
        out_specs=blk((rows, S5_W)),
        out_shape=jax.ShapeDtypeStruct((g, rows, S5_W), BF16),
        scratch_shapes=[pltpu.VMEM((rows, S5_N), F32)] * 4,
        compiler_params=_cparams("parallel"),
        name="s5_apply",
    )(ut, strip, wzr, wzi, ptr, pti, ar, ai)


def _s5_glu_kernel(h_ref, y_ref, g_ref, d_ref, w_ref, b_ref, o_ref, yv_ref, yb_ref, *, tn):
    j = pl.program_id(1)

    @pl.when(j == 0)
    def _():
        g = g_ref[...]
        dv = d_ref[...]

        def body(b, carry):
            rs = pl.ds(pl.multiple_of(b * GLU_ROWS, GLU_ROWS), GLU_ROWS)
            y = y_ref[rs, :].astype(F32) + dv * _rms(h_ref[rs, :], g)
            y = 0.5 * y * (1.0 + jnp.tanh(math.sqrt(2.0 / math.pi) * (y + 0.044715 * (y * y * y))))
            yv_ref[rs, :] = y
            yb_ref[rs, :] = y.astype(BF16)
            return carry

        lax.fori_loop(0, h_ref.shape[0] // GLU_ROWS, body, 0)

    cols = pl.ds(pl.multiple_of(j * tn, tn), tn)
    z = _dot(yb_ref[...], w_ref[...]) + b_ref[...]
    o_ref[...] = h_ref[:, cols] + yv_ref[:, cols] * _sigmoid(z)


def _s5_glu(h, y, g, dvec, w_glu, o, b_glu, *, tm, tn):
    n, d = h.shape
    return pl.pallas_call(
        functools.partial(_s5_glu_kernel, tn=tn),
        grid=(n // tm, d // tn),
        in_specs=[
            pl.BlockSpec((tm, d), lambda i, j: (i, 0)),
            pl.BlockSpec((tm, d), lambda i, j: (i, 0)),
            pl.BlockSpec((1, d), lambda i, j: (0, 0)),
            pl.BlockSpec((1, d), lambda i, j: (0, 0)),
            pl.BlockSpec((None, d, tn), lambda i, j: (o, 0, j)),
            pl.BlockSpec((1, tn), lambda i, j: (0, j)),
        ],
        out_specs=pl.BlockSpec((tm, tn), lambda i, j: (i, j)),
        out_shape=jax.ShapeDtypeStruct((n, d), F32),
        scratch_shapes=[pltpu.VMEM((tm, d), F32), pltpu.VMEM((tm, d), BF16)],
        compiler_params=_cparams("parallel", "arbitrary"),
        name="s5_glu",
    )(h, y, g, dvec, w_glu, b_glu)


def _even_layer(h, g_mix, w_main, w_gate, e, m_conv_w, m_conv_b, m_b_igate, m_b_fgate, m_head_norm,
                g_w_gate, g_b_gate, g_head_norm, w_out, *, bsz, seq, tm, tb):
    n, d = h.shape
    pm, gates = _inproj(h, g_mix.reshape(1, d), w_main, w_gate, e, tm=tm, tn=1024)
    gates_t = gates[:, 0:2 * M_HEADS].reshape(bsz, seq // CHUNK, CHUNK, 2, M_HEADS).transpose(3, 0, 1, 4, 2)
    gates_t = gates_t.reshape(2, bsz, seq // CHUNK * (M_HEADS // 2), 2 * CHUNK)
    bias = jnp.repeat(jnp.stack([m_b_igate, m_b_fgate]).astype(F32), CHUNK, axis=-1)
    bias = bias.reshape(2, M_HEADS // 2, 2 * CHUNK)
    hm = _mlstm(pm, gates_t, m_conv_w, m_conv_b.reshape(1, -1), bias, m_head_norm.reshape(1, -1),
                bsz=bsz, seq=seq, tb=tb)
    wg_pad = jnp.zeros((GATE_LANES, G_HEADS * G_DK), BF16).at[16:16 + G_RANK].set(g_w_gate.astype(BF16))
    hg = _gla(pm, gates, wg_pad, g_b_gate.reshape(1, -1), g_head_norm.reshape(1, -1),
              bsz=bsz, seq=seq, tb=tb)
    return _outproj(hm, hg, w_out, e, h, tm=tm, tn=1024)


def _odd_layer(h, u, g_mix, lam_re, lam_im, log_dt, b_re, b_im, c_re, c_im, dvec, w_glu, o, b_glu,
               *, bsz, seq, tm):
    n, d = h.shape
    groups = d // S5_P
    nchunks = seq // CHUNK
    ops = _s5_gen(lam_re, lam_im, log_dt, jnp.swapaxes(b_re, 1, 2), jnp.swapaxes(b_im, 1, 2),
                  jnp.swapaxes(c_re, 1, 2), jnp.swapaxes(c_im, 1, 2))
    ut = u.reshape(bsz, nchunks, CHUNK, groups, S5_P).transpose(3, 1, 0, 2, 4)
    ut = ut.reshape(groups, nchunks * bsz, S5_W)
    yt = _s5_apply(ut, *ops, bsz=bsz, nchunks=nchunks)
    y = yt.reshape(groups, nchunks, bsz, CHUNK, S5_P).transpose(2, 1, 3, 0, 4).reshape(n, d)
    return _s5_glu(h, y, g_mix.reshape(1, d), dvec.reshape(1, d), w_glu, o, b_glu.reshape(1, d),
                   tm=tm, tn=1024)


def _regroup_kernel(w_ref, o_ref, *, segments):
    for src, dst, n in segments:
        if src is None:
            o_ref[:, dst:dst + n] = jnp.zeros((o_ref.shape[0], n), o_ref.dtype)
        else:
            o_ref[:, dst:dst + n] = w_ref[:, src:src + n].astype(o_ref.dtype)


def _regroup_cols(w, segments, width, dtype, *, rows):
    nl, r, c = w.shape
    return pl.pallas_call(
        functools.partial(_regroup_kernel, segments=segments),
        grid=(nl, r // rows),
        in_specs=[pl.BlockSpec((None, rows, c), lambda l, i: (l, i, 0))],
        out_specs=pl.BlockSpec((None, rows, width), lambda l, i: (l, i, 0)),
        out_shape=jax.ShapeDtypeStruct((nl, r, width), dtype),
        compiler_params=_cparams("parallel", "parallel"),
        name="regroup_cols",
    )(w)


def _tile_up_kernel(w_ref, o_ref, *, dff, tk):
    for k in range(o_ref.shape[0]):
        n = min(tk, dff - k * tk)
        for half, base in ((0, 0), (1, dff)):
            o_ref[k, :, half * tk:half * tk + n] = w_ref[:, base + k * tk:base + k * tk + n].astype(o_ref.dtype)
            if n < tk:
                o_ref[k, :, half * tk + n:(half + 1) * tk] = jnp.zeros((o_ref.shape[1], tk - n), o_ref.dtype)


def _tile_up(w, tk, *, rows):
    nl, d, two_ff = w.shape
    dff = two_ff // 2
    nk = -(-dff // tk)
    return pl.pallas_call(
        functools.partial(_tile_up_kernel, dff=dff, tk=tk),
        grid=(nl, d // rows),
        in_specs=[pl.BlockSpec((None, rows, two_ff), lambda l, i: (l, i, 0))],
        out_specs=pl.BlockSpec((None, nk, rows, 2 * tk), lambda l, i: (l, 0, i, 0)),
        out_shape=jax.ShapeDtypeStruct((nl, nk, d, 2 * tk), BF16),
        compiler_params=_cparams("parallel", "parallel"),
        name="tile_up",
    )(w)


def _prep_ffn(ffn_w_up, ffn_conv_w, ffn_conv_b, ffn_w_down, tk):
    dff = ffn_w_down.shape[1]
    pad = -dff % tk
    dffp = dff + pad

    def padded(a, dtype):
        lead = a.shape[:-1]
        gv = jnp.pad(a.reshape(lead + (2, dff)), [(0, 0)] * len(lead) + [(0, 0), (0, pad)])
        return gv.astype(dtype).reshape(lead + (2 * dffp,))

    wup = _tile_up(ffn_w_up, tk, rows=128)
    cw = padded(ffn_conv_w, F32)
    cb = padded(ffn_conv_b, F32)[:, None, :]
    wdn = jnp.pad(ffn_w_down, [(0, 0), (0, pad), (0, 0)]).astype(BF16)
    return wup, cw, cb, wdn


def kernel(x, norm_mix, norm_ffn, ffn_w_up, ffn_conv_w, ffn_conv_b, ffn_w_down, norm_final,
           w_in, m_conv_w, m_conv_b, m_b_igate, m_b_fgate, m_head_norm,
           g_w_gate, g_b_gate, g_head_norm, w_out,
           s5_lambda_re, s5_lambda_im, s5_log_dt, s5_b_re, s5_b_im, s5_c_re, s5_c_im,
           s5_d, s5_w_glu, s5_b_glu):
    return _forward(x, norm_mix, norm_ffn, ffn_w_up, ffn_conv_w, ffn_conv_b, ffn_w_down, norm_final,
                    w_in, m_conv_w, m_conv_b, m_b_igate, m_b_fgate, m_head_norm,
                    g_w_gate, g_b_gate, g_head_norm, w_out,
                    s5_lambda_re, s5_lambda_im, s5_log_dt, s5_b_re, s5_b_im, s5_c_re, s5_c_im,
                    s5_d, s5_w_glu, s5_b_glu, tm=512, tf=512, tb=512)


def _forward(x, norm_mix, norm_ffn, ffn_w_up, ffn_conv_w, ffn_conv_b, ffn_w_down, norm_final,
             w_in, m_conv_w, m_conv_b, m_b_igate, m_b_fgate, m_head_norm,
             g_w_gate, g_b_gate, g_head_norm, w_out,
             s5_lambda_re, s5_lambda_im, s5_log_dt, s5_b_re, s5_b_im, s5_c_re, s5_c_im,
             s5_d, s5_w_glu, s5_b_glu, *, tm, tf, tb):
    bsz, seq, d = x.shape
    depth = norm_mix.shape[0]
    n = bsz * seq
    tmd = min(2 * tm, n)
    h = x.reshape(n, d)

    wup, cw, cb, wdn = _prep_ffn(ffn_w_up, ffn_conv_w, ffn_conv_b, ffn_w_down, 512)
    c0 = 2 * M_HEADS * M_DK + 2 * M_HEADS * M_DV
    c1 = c0 + 2 * M_HEADS
    c2 = c1 + 2 * G_HEADS * G_DK + 2 * G_HEADS * G_DV
    ng = (c1 - c0) + G_RANK
    w_main = _regroup_cols(w_in, ((0, 0, c0), (c1, c0, c2 - c1)), c0 + c2 - c1, BF16, rows=256)
    w_gate = _regroup_cols(w_in, ((c0, 0, c1 - c0), (c2, c1 - c0, G_RANK), (None, ng, GATE_LANES - ng)),
                           GATE_LANES, BF16, rows=256)
    w_out_b = w_out.astype(BF16)
    w_glu_b = s5_w_glu.astype(BF16)

    for layer in range(depth):
        if layer % 2 == 0:
            e = layer // 2
            h = _even_layer(h, norm_mix[layer], w_main, w_gate, e, m_conv_w[e], m_conv_b[e],
                            m_b_igate[e], m_b_fgate[e], m_head_norm[e], g_w_gate[e], g_b_gate[e],
                            g_head_norm[e], w_out_b, bsz=bsz, seq=seq, tm=tmd, tb=tb)
        else:
            o = layer // 2
            h = _odd_layer(h, u, norm_mix[layer], s5_lambda_re[o], s5_lambda_im[o], s5_log_dt[o],
                           s5_b_re[o], s5_b_im[o], s5_c_re[o], s5_c_im[o], s5_d[o], w_glu_b, o,
                           s5_b_glu[o], bsz=bsz, seq=seq, tm=tm)
        if layer == depth - 1:
            post, gpost = "inplace", norm_final
        elif layer % 2 == 0:
            post, gpost = "extra", norm_mix[layer + 1]
        else:
            post, gpost = None, norm_ffn[layer]
        outs = _ffn(h, norm_ffn[layer].reshape(1, d), wup, cw, cb, wdn, layer, gpost.reshape(1, d), post,
                    seq=seq, tm=tf, tk=512, rb=min(256, tf))
        h = outs[0]
        u = outs[1] if post == "extra" else None
    return h.reshape(bsz, seq, d)
```

```python
import functools
import math

import jax
import jax.numpy as jnp
from jax import lax
from jax.experimental import pallas as pl
from jax.experimental.pallas import tpu as pltpu

F32 = jnp.float32
BF16 = jnp.bfloat16
HI = lax.Precision.HIGHEST

EPS = 1e-6
CHUNK = 64
M_HEADS, M_DK, M_DV, M_CONV = 8, 64, 128, 4
G_HEADS, G_DK, G_DV, G_RANK, G_TAU = 4, 128, 256, 16, 16.0
S5_P, S5_N = 16, 64
FFN_CONV = 3
GATE_LANES = 128
SUBCHUNK = 16

VMEM_LIMIT = 56 * 1024 * 1024


def _cparams(*sem):
    return pltpu.CompilerParams(dimension_semantics=sem, vmem_limit_bytes=VMEM_LIMIT)


def _rms(x, g):
    return x * lax.rsqrt(jnp.mean(x * x, axis=-1, keepdims=True) + EPS) * g


def _sigmoid(x):
    return 0.5 + 0.5 * jnp.tanh(0.5 * x)


def _log_sigmoid(x):
    return jnp.minimum(x, 0.0) - jnp.log(1.0 + jnp.exp(-jnp.abs(x)))


def _dot(a, b, **kw):
    return jnp.dot(a, b, preferred_element_type=F32, **kw)


def _dot_nt(a, b, **kw):
    return lax.dot_general(a, b, (((1,), (1,)), ((), ())), preferred_element_type=F32, **kw)


def _dot_tn(a, b, **kw):
    return lax.dot_general(a, b, (((0,), (0,)), ((), ())), preferred_element_type=F32, **kw)


def _inproj_kernel(x_ref, g_ref, w_ref, wg_ref, o_ref, og_ref, xn_ref):
    @pl.when(pl.program_id(1) == 0)
    def _():
        _rms_rows_to(xn_ref, 0, x_ref, g_ref, x_ref.shape[0])
        og_ref[...] = _dot(xn_ref[...], wg_ref[...])

    o_ref[...] = _dot(xn_ref[...], w_ref[...]).astype(o_ref.dtype)


def _inproj(h, g, w_main, w_gate, e, *, tm, tn):
    n, d = h.shape
    wn = w_main.shape[2]
    return pl.pallas_call(
        _inproj_kernel,
        grid=(n // tm, wn // tn),
        in_specs=[
            pl.BlockSpec((tm, d), lambda i, j: (i, 0)),
            pl.BlockSpec((1, d), lambda i, j: (0, 0)),
            pl.BlockSpec((None, d, tn), lambda i, j: (e, 0, j)),
            pl.BlockSpec((None, d, GATE_LANES), lambda i, j: (e, 0, 0)),
        ],
        out_specs=[
            pl.BlockSpec((tm, tn), lambda i, j: (i, j)),
            pl.BlockSpec((tm, GATE_LANES), lambda i, j: (i, 0)),
        ],
        out_shape=[
            jax.ShapeDtypeStruct((n, wn), BF16),
            jax.ShapeDtypeStruct((n, GATE_LANES), F32),
        ],
        scratch_shapes=[pltpu.VMEM((tm, d), BF16)],
        compiler_params=_cparams("parallel", "arbitrary"),
        name="inproj",
    )(h, g, w_main, w_gate)


def _mlstm_select():
    npair = M_HEADS // 2
    npc = 2 * npair * 2 * CHUNK
    jrow = lax.broadcasted_iota(jnp.int32, (96, npc + M_HEADS * M_DV), 0)
    ncol = lax.broadcasted_iota(jnp.int32, (96, npc + M_HEADS * M_DV), 1)
    half = jrow // 48
    j16 = jrow % 16
    sel_pair = (ncol < npc) & (ncol // (2 * CHUNK) == j16) & ((ncol % (2 * CHUNK)) // CHUNK == half) & (j16 < 8)
    sel_em = (ncol >= npc) & (j16 >= 8) & (j16 < 12) & ((ncol - npc) // M_DV == 2 * (j16 - 8) + half)
    return (sel_pair | sel_em).astype(F32)


def _mlstm_kernel(mq_ref, mk_ref, mv_ref, mo_ref, li_ref, lf_ref, cw_ref, cb_ref, bi_ref, bf_ref, hn_ref,
                  sel_ref, o_ref, qk_scr, qkc_scr, c_scr, m_scr, row_scr, rep_scr, *, tb):
    @pl.when(pl.program_id(1) == 0)
    def _():
        qk_scr[0:8, :] = jnp.zeros((8, 2 * M_HEADS * M_DK), F32)
        c_scr[...] = jnp.zeros(c_scr.shape, F32)
        m_scr[...] = jnp.zeros(m_scr.shape, F32)

    nqk = M_HEADS * M_DK
    qk_scr[8:8 + tb, 0:nqk] = mq_ref[...].astype(F32)
    qk_scr[8:8 + tb, nqk:2 * nqk] = mk_ref[...].astype(F32)
    conv = cb_ref[...] + cw_ref[0:1, :] * qk_scr[5:5 + tb, :]
    for j in range(1, M_CONV):
        conv = conv + cw_ref[j:j + 1, :] * qk_scr[5 + j:5 + j + tb, :]
    tail = qk_scr[tb:tb + 8, :]
    qkc_scr[...] = conv * _sigmoid(conv)
    qk_scr[0:8, :] = tail

    npair = M_HEADS // 2
    lane = lax.broadcasted_iota(jnp.int32, (1, 2 * CHUNK), 1)
    lo_half = lane < CHUNK
    pos = lane % CHUNK
    trow = lax.broadcasted_iota(jnp.int32, (CHUNK, 2 * CHUNK), 0)
    causal2 = (lax.broadcasted_iota(jnp.int32, (CHUNK, 2 * CHUNK), 1) % CHUNK) <= trow
    r2 = lax.broadcasted_iota(jnp.int32, (2 * CHUNK, 2 * CHUNK), 0)
    c2 = lax.broadcasted_iota(jnp.int32, (2 * CHUNK, 2 * CHUNK), 1)
    tri2 = ((r2 // CHUNK == c2 // CHUNK) & (r2 <= c2)).astype(F32)
    row_dk = lax.broadcasted_iota(jnp.int32, (2 * M_DK, 1), 0)
    row_lo = row_dk < M_DK
    ones_v = jnp.ones((CHUNK, M_DV), BF16)
    zeros_v = jnp.zeros((CHUNK, 2 * M_DV), BF16)
    ones_sum = jnp.ones((M_DV, M_DV), BF16)
    ones_n = jnp.ones((CHUNK, M_DV), F32)
    scale = M_DK ** -0.5
    sel = sel_ref[...]

    def exact3(x):
        hi = x.astype(BF16).astype(F32)
        mid = (x - hi).astype(BF16).astype(F32)
        return hi, mid, x - hi - mid

    def half_max(x):
        m0 = jnp.max(jnp.where(lo_half, x, -jnp.inf), axis=-1, keepdims=True)
        m1 = jnp.max(jnp.where(lo_half, -jnp.inf, x), axis=-1, keepdims=True)
        return jnp.where(lo_half, m0, m1)

    nchunk = tb // CHUNK
    li = li_ref[0] + bi_ref[...]
    b = _dot(_log_sigmoid(lf_ref[0] + bf_ref[...]), tri2, precision=HI)
    a = li - b
    cm = a
    for sh in (1, 2, 4, 8, 16, 32):
        cm = jnp.maximum(cm, jnp.where(pos >= sh, pltpu.roll(cm, sh, 1), -jnp.inf))
    b_last = jnp.where(lo_half, b[:, CHUNK - 1:CHUNK], b[:, 2 * CHUNK - 1:2 * CHUNK])
    g = b_last + a
    gmax = half_max(g)
    m = m_scr[...]
    m_starts = []
    for c in range(nchunk):
        m_starts.append(m)
        m = jnp.maximum(b_last[npair * c:npair * (c + 1)] + m, gmax[npair * c:npair * (c + 1)])
    m_scr[...] = m
    m_prev = jnp.concatenate(m_starts, axis=0)
    m_new = jnp.maximum(b_last + m_prev, gmax)
    m_out = b + jnp.maximum(m_prev, cm)
    e1 = b - m_out
    terms = exact3(e1) + exact3(jnp.exp(e1 + m_prev)) + exact3(jnp.exp(-m_out))
    wk = jnp.exp(g - m_new)
    decay = jnp.exp(b_last + m_prev - m_new)
    zero4 = jnp.zeros((npair, 2 * CHUNK), F32)
    for c in range(nchunk):
        rs = slice(npair * c, npair * (c + 1))
        row_scr[c, 0:npair] = a[rs]
        row_scr[c, npair:2 * npair] = wk[rs]
        row_scr[c, 2 * npair:3 * npair] = decay[rs]
        q48 = jnp.concatenate([x for t in range(3) for x in (terms[t][rs], terms[3 + t][rs], terms[6 + t][rs], zero4)],
                              axis=0)
        qfull = jnp.concatenate([q48[:, 0:CHUNK], q48[:, CHUNK:2 * CHUNK]], axis=0)
        rep_scr[c] = _dot_tn(qfull, sel)

    def chunk(c, carry):
        rows = pl.ds(pl.multiple_of(c * CHUNK, CHUNK), CHUNK)
        a = row_scr[c, 0:npair]
        wk = row_scr[c, npair:2 * npair]
        decay = row_scr[c, 2 * npair:3 * npair]
        rep = rep_scr[c]
        for p in range(npair):
            qp = qkc_scr[rows, 2 * M_DK * p:2 * M_DK * (p + 1)] * scale
            kt = qkc_scr[rows, nqk + 2 * M_DK * p:nqk + 2 * M_DK * (p + 1)].T
            kt2 = jnp.concatenate([jnp.where(row_lo, kt, 0.0), jnp.where(row_lo, 0.0, kt)], axis=1)
            e1c = rep[:, 2 * CHUNK * p:2 * CHUNK * (p + 1)]
            wic = rep[:, 2 * CHUNK * (npair + p):2 * CHUNK * (npair + p + 1)]
            dexp = jnp.exp(jnp.where(causal2, a[p:p + 1, :] + e1c, -jnp.inf))
            smat = (_dot(qp.astype(BF16), kt2.astype(BF16)) * dexp).astype(BF16)
            v0 = mv_ref[rows, M_DV * 2 * p:M_DV * (2 * p + 1)]
            v1 = mv_ref[rows, M_DV * (2 * p + 1):M_DV * (2 * p + 2)]
            vbd = jnp.concatenate([jnp.concatenate([v0, ones_v, zeros_v], axis=1),
                                   jnp.concatenate([zeros_v, v1, ones_v], axis=1)], axis=0)
            cbd = c_scr[p]
            numext = _dot(smat, vbd) + _dot((qp * wic).astype(BF16), cbd.astype(BF16))
            for e in range(2):
                h = 2 * p + e
                num = numext[:, 2 * M_DV * e:2 * M_DV * e + M_DV]
                den = numext[:, 2 * M_DV * e + M_DV:2 * M_DV * (e + 1)]
                emc = rep[:, 2 * npair * 2 * CHUNK + M_DV * h:2 * npair * 2 * CHUNK + M_DV * (h + 1)]
                hh = num / jnp.maximum(jnp.abs(den), emc)
                sq = hh * hh
                sq_hi = sq.astype(BF16)
                ms = (_dot(sq_hi, ones_sum) + _dot((sq - sq_hi.astype(F32)).astype(BF16), ones_sum)) * (1.0 / M_DV)
                y = hh * lax.rsqrt(ms + EPS) * hn_ref[:, M_DV * h:M_DV * (h + 1)]
                og = _sigmoid(mo_ref[rows, M_DV * h:M_DV * (h + 1)].astype(F32))
                o_ref[rows, M_DV * h:M_DV * (h + 1)] = (og * y).astype(o_ref.dtype)
            wkp = wk[p:p + 1, :]
            kts = jnp.where(row_lo, kt * wkp[:, 0:CHUNK], kt * wkp[:, CHUNK:2 * CHUNK])
            kts = kts.astype(BF16).astype(F32)
            uc = _dot(kts, jnp.concatenate([v0, v1], axis=1).astype(F32))
            un = _dot(kts, ones_n)
            zc = jnp.zeros((2 * M_DK, M_DV), F32)
            upd = jnp.where(row_lo, jnp.concatenate([uc[:, 0:M_DV], un, zc, zc], axis=1),
                            jnp.concatenate([zc, zc, uc[:, M_DV:2 * M_DV], un], axis=1))
            dp = decay[p:p + 1, :]
            dcol = jnp.where(row_lo, dp[:, 0:1], dp[:, CHUNK:CHUNK + 1])
            c_scr[p] = dcol * cbd + upd
        return carry

    lax.fori_loop(0, tb // CHUNK, chunk, 0, unroll=4)


def _mlstm(pm, gates_t, conv_w, conv_b, bias, hnorm, *, bsz, seq, tb):
    n = pm.shape[0]
    nt = seq // tb
    nqk = M_HEADS * M_DK
    nv = M_HEADS * M_DV
    sel = _mlstm_select()
    rpb = tb // CHUNK * (M_HEADS // 2)
    bias_i = jnp.tile(bias[0], (tb // CHUNK, 1))
    bias_f = jnp.tile(bias[1], (tb // CHUNK, 1))
    return pl.pallas_call(
        functools.partial(_mlstm_kernel, tb=tb),
        grid=(bsz, nt),
        in_specs=[
            pl.BlockSpec((tb, nqk), lambda b, t: (b * nt + t, 0)),
            pl.BlockSpec((tb, nqk), lambda b, t: (b * nt + t, 1)),
            pl.BlockSpec((tb, nv), lambda b, t: (b * nt + t, 1)),
            pl.BlockSpec((tb, nv), lambda b, t: (b * nt + t, 2)),
            pl.BlockSpec((None, 1, rpb, 2 * CHUNK), lambda b, t: (0, b, t, 0)),
            pl.BlockSpec((None, 1, rpb, 2 * CHUNK), lambda b, t: (1, b, t, 0)),
            pl.BlockSpec((M_CONV, 2 * nqk), lambda b, t: (0, 0)),
            pl.BlockSpec((1, 2 * nqk), lambda b, t: (0, 0)),
            pl.BlockSpec((rpb, 2 * CHUNK), lambda b, t: (0, 0)),
            pl.BlockSpec((rpb, 2 * CHUNK), lambda b, t: (0, 0)),
            pl.BlockSpec((1, nv), lambda b, t: (0, 0)),
            pl.BlockSpec(sel.shape, lambda b, t: (0, 0)),
        ],
        out_specs=pl.BlockSpec((tb, nv), lambda b, t: (b * nt + t, 0)),
        out_shape=jax.ShapeDtypeStruct((n, nv), BF16),
        scratch_shapes=[
            pltpu.VMEM((tb + 8, 2 * nqk), F32),
            pltpu.VMEM((tb, 2 * nqk), F32),
            pltpu.VMEM((M_HEADS // 2, 2 * M_DK, 4 * M_DV), F32),
            pltpu.VMEM((M_HEADS // 2, 2 * CHUNK), F32),
            pltpu.VMEM((tb // CHUNK, 2 * M_HEADS, 2 * CHUNK), F32),
            pltpu.VMEM((tb // CHUNK, CHUNK, sel.shape[1]), F32),
        ],
        compiler_params=_cparams("parallel", "arbitrary"),
        name="mlstm",
    )(pm, pm, pm, pm, gates_t, gates_t, conv_w, conv_b, bias_i, bias_f, hnorm, sel)


def _gla_kernel(gq_ref, gk_ref, gv_ref, gg_ref, gc_ref, wg_ref, bg_ref, hn_ref, o_ref, s_scr, bc_scr, *, tb):
    @pl.when(pl.program_id(1) == 0)
    def _():
        s_scr[...] = jnp.zeros(s_scr.shape, F32)

    row = lax.broadcasted_iota(jnp.int32, (CHUNK, CHUNK), 0)
    col = lax.broadcasted_iota(jnp.int32, (CHUNK, CHUNK), 1)
    tril = row >= col
    tri = tril.astype(F32)
    rowk = lax.broadcasted_iota(jnp.int32, (CHUNK, 1), 0)
    nsub = CHUNK // SUBCHUNK
    scale = G_DK ** -0.5

    pre = _dot(gc_ref[...].astype(BF16), wg_ref[...]) + bg_ref[...]
    la = _log_sigmoid(pre) * (1.0 / G_TAU)
    for c in range(tb // CHUNK):
        bc_scr[CHUNK * c:CHUNK * (c + 1), :] = _dot(tri, la[CHUNK * c:CHUNK * (c + 1), :], precision=HI)

    def chunk(c, carry):
        r0 = pl.multiple_of(c * CHUNK, CHUNK)
        rows = pl.ds(r0, CHUNK)
        bc_all = bc_scr[rows, :]
        for h in range(G_HEADS):
            ks = slice(G_DK * h, G_DK * (h + 1))
            vs = slice(G_DV * h, G_DV * (h + 1))
            bc = bc_all[:, ks]
            q = gq_ref[rows, ks].astype(F32) * scale
            k = gk_ref[rows, ks].astype(F32)
            v = gv_ref[rows, vs]
            st = s_scr[h]
            o = _dot_nt((q * jnp.exp(bc)).astype(BF16), st.astype(BF16))
            cblk = jnp.concatenate(
                [jnp.broadcast_to(bc[SUBCHUNK * i:SUBCHUNK * i + 1, :], (SUBCHUNK, G_DK))
                 for i in range(nsub)], axis=0)
            qt = (q * jnp.exp(bc - cblk)).astype(BF16)
            blocks = []
            for i in range(nsub):
                ci = bc[SUBCHUNK * i:SUBCHUNK * i + 1, :]
                kt = jnp.where(rowk < SUBCHUNK * (i + 1), k * jnp.exp(ci - bc), 0.0).astype(BF16)
                blocks.append(_dot_nt(qt[SUBCHUNK * i:SUBCHUNK * (i + 1), :], kt))
            a = jnp.where(tril, jnp.concatenate(blocks, axis=0), 0.0)
            o = o + _dot(a.astype(BF16), v)
            y = o * lax.rsqrt(jnp.mean(o * o, axis=-1, keepdims=True) + EPS) * hn_ref[:, vs]
            gg = gg_ref[rows, vs].astype(F32)
            o_ref[rows, vs] = (gg * _sigmoid(gg) * y).astype(o_ref.dtype)
            last = bc[CHUNK - 1:CHUNK, :]
            kd = (k * jnp.exp(last - bc)).astype(BF16)
            s_scr[h] = st * jnp.exp(last) + _dot_tn(v, kd)
        return carry

    lax.fori_loop(0, tb // CHUNK, chunk, 0, unroll=4)


def _gla(pm, gates, wg_pad, bg, hnorm, *, bsz, seq, tb):
    n = pm.shape[0]
    nt = seq // tb
    nqk = G_HEADS * G_DK
    nv = G_HEADS * G_DV
    rowmap = lambda b, t: (b * nt + t, 0)
    return pl.pallas_call(
        functools.partial(_gla_kernel, tb=tb),
        grid=(bsz, nt),
        in_specs=[
            pl.BlockSpec((tb, nqk), lambda b, t: (b * nt + t, 6)),
            pl.BlockSpec((tb, nqk), lambda b, t: (b * nt + t, 7)),
            pl.BlockSpec((tb, nv), lambda b, t: (b * nt + t, 4)),
            pl.BlockSpec((tb, nv), lambda b, t: (b * nt + t, 5)),
            pl.BlockSpec((tb, GATE_LANES), rowmap),
            pl.BlockSpec((GATE_LANES, nqk), lambda b, t: (0, 0)),
            pl.BlockSpec((1, nqk), lambda b, t: (0, 0)),
            pl.BlockSpec((1, nv), lambda b, t: (0, 0)),
        ],
        out_specs=pl.BlockSpec((tb, nv), rowmap),
        out_shape=jax.ShapeDtypeStruct((n, nv), BF16),
        scratch_shapes=[pltpu.VMEM((G_HEADS, G_DV, G_DK), F32), pltpu.VMEM((tb, nqk), F32)],
        compiler_params=_cparams("parallel", "arbitrary"),
        name="gla",
    )(pm, pm, pm, pm, gates, wg_pad, bg, hnorm)


def _outproj_kernel(hm_ref, hg_ref, w1_ref, w2_ref, h_ref, o_ref):
    o_ref[...] = h_ref[...] + _dot(hm_ref[...], w1_ref[...]) + _dot(hg_ref[...], w2_ref[...])


def _outproj(hm, hg, w_out, e, h, *, tm, tn):
    n, d = h.shape
    kh = hm.shape[1]
    return pl.pallas_call(
        _outproj_kernel,
        grid=(n // tm, d // tn),
        in_specs=[
            pl.BlockSpec((tm, kh), lambda i, j: (i, 0)),
            pl.BlockSpec((tm, kh), lambda i, j: (i, 0)),
            pl.BlockSpec((None, kh, tn), lambda i, j: (e, 0, j)),
            pl.BlockSpec((None, kh, tn), lambda i, j: (e, 1, j)),
            pl.BlockSpec((tm, tn), lambda i, j: (i, j)),
        ],
        out_specs=pl.BlockSpec((tm, tn), lambda i, j: (i, j)),
        out_shape=jax.ShapeDtypeStruct((n, d), F32),
        compiler_params=_cparams("parallel", "arbitrary"),
        name="outproj",
    )(hm, hg, w_out, w_out, h)


FFN_HALO = 16
MXU_COLS = 256
NORM_ROWS = 128
GLU_ROWS = 64


def _rms_rows_to(dst_ref, dst_off, src_ref, g_ref, rows):
    g = g_ref[...]
    for r0 in range(0, rows, NORM_ROWS):
        nr = min(NORM_ROWS, rows - r0)
        dst_ref[dst_off + r0:dst_off + r0 + nr, :] = _rms(src_ref[r0:r0 + nr, :], g).astype(dst_ref.dtype)


def _ffn_kernel(h_ref, halo_ref, g_ref, wg_ref, wv_ref, cwg_ref, cwv_ref, cbg_ref, cbv_ref, wdn_ref,
                gpost_ref, *rest, tm, tk, rb, tiles_per_seq, post):
    if post == "extra":
        o_ref, u_ref, xn_ref, a_ref = rest
    else:
        o_ref, xn_ref, a_ref = rest
    i = pl.program_id(0)

    @pl.when(pl.program_id(1) == 0)
    def _():
        _rms_rows_to(xn_ref, 0, halo_ref, g_ref, FFN_HALO)
        _rms_rows_to(xn_ref, FFN_HALO, h_ref, g_ref, tm)
        o_ref[...] = h_ref[...]

    keep = ((i % tiles_per_seq) != 0).astype(F32)

    def up(r0, nr):
        for c0 in range(0, tk, MXU_COLS):
            a_ref[r0:r0 + nr, c0:c0 + MXU_COLS] = _dot(xn_ref[r0:r0 + nr, :], wg_ref[:, c0:c0 + MXU_COLS])
            a_ref[r0:r0 + nr, tk + c0:tk + c0 + MXU_COLS] = _dot(xn_ref[r0:r0 + nr, :], wv_ref[:, c0:c0 + MXU_COLS])

    up(0, FFN_HALO + rb)
    a_ref[0:FFN_HALO, :] = a_ref[0:FFN_HALO, :] * keep
    for r in range(1, tm // rb):
        up(FFN_HALO + r * rb, rb)
    w0 = jnp.concatenate([cwg_ref[0:1, :], cwv_ref[0:1, :]], axis=-1)
    w1 = jnp.concatenate([cwg_ref[1:2, :], cwv_ref[1:2, :]], axis=-1)
    w2 = jnp.concatenate([cwg_ref[2:3, :], cwv_ref[2:3, :]], axis=-1)
    cb = jnp.concatenate([cbg_ref[...], cbv_ref[...]], axis=-1)
    for r in range(tm // rb):
        r0 = FFN_HALO + r * rb
        c = (w2 * a_ref[r0:r0 + rb, :] + w1 * a_ref[r0 - 1:r0 - 1 + rb, :]
             + w0 * a_ref[r0 - 2:r0 - 2 + rb, :] + cb)
        hg = 0.5 * c[:, 0:tk]
        act = ((hg + hg * jnp.tanh(hg)) * c[:, tk:2 * tk]).astype(BF16)
        o_ref[r * rb:(r + 1) * rb, :] += _dot(act, wdn_ref[...])

    if post is not None:
        @pl.when(pl.program_id(1) == pl.num_programs(1) - 1)
        def _():
            _rms_rows_to(u_ref if post == "extra" else o_ref, 0, o_ref, gpost_ref, tm)


def _ffn(h, g, wup, cw, cb, wdn, layer, gpost, post, *, seq, tm, tk, rb):
    n, d = h.shape
    nk = wdn.shape[1] // tk
    hb = tm // FFN_HALO
    out_specs = [pl.BlockSpec((tm, d), lambda i, k: (i, 0))]
    out_shape = [jax.ShapeDtypeStruct((n, d), F32)]
    if post == "extra":
        out_specs.append(pl.BlockSpec((tm, d), lambda i, k: (i, 0)))
        out_shape.append(jax.ShapeDtypeStruct((n, d), BF16))
    return pl.pallas_call(
        functools.partial(_ffn_kernel, tm=tm, tk=tk, rb=rb, tiles_per_seq=seq // tm, post=post),
        grid=(n // tm, nk),
        in_specs=[
            pl.BlockSpec((tm, d), lambda i, k: (i, 0)),
            pl.BlockSpec((FFN_HALO, d), lambda i, k: (jnp.maximum(i * hb - 1, 0), 0)),
            pl.BlockSpec((1, d), lambda i, k: (0, 0)),
            pl.BlockSpec((None, None, None, d, tk), lambda i, k: (layer, k, 0, 0, 0)),
            pl.BlockSpec((None, None, None, d, tk), lambda i, k: (layer, k, 1, 0, 0)),
            pl.BlockSpec((None, FFN_CONV, tk), lambda i, k: (layer, 0, k)),
            pl.BlockSpec((None, FFN_CONV, tk), lambda i, k: (layer, 0, nk + k)),
            pl.BlockSpec((None, 1, tk), lambda i, k: (layer, 0, k)),
            pl.BlockSpec((None, 1, tk), lambda i, k: (layer, 0, nk + k)),
            pl.BlockSpec((None, tk, d), lambda i, k: (layer, k, 0)),
            pl.BlockSpec((1, d), lambda i, k: (0, 0)),
        ],
        out_specs=out_specs,
        out_shape=out_shape,
        scratch_shapes=[
            pltpu.VMEM((tm + FFN_HALO, d), BF16),
            pltpu.VMEM((tm + FFN_HALO, 2 * tk), F32),
        ],
        compiler_params=_cparams("parallel", "arbitrary"),
        name="convffn",
    )(h, h, g, wup, wup, cw, cw, cb, cb, wdn, gpost)


S5_TILE = SUBCHUNK * S5_P
S5_W = CHUNK * S5_P


def _s5_gen_kernel(lr_ref, li_ref, lrc_ref, lic_ref, dt_ref, bre_ref, bim_ref, cre_ref, cim_ref,
                   strip_ref, wzr_ref, wzi_ref, ptr_ref, pti_ref, ar_ref, ai_ref):
    dt = jnp.exp(dt_ref[0])
    lam_r = lr_ref[0]
    lam_i = li_ref[0]
    xr = lam_r * dt
    xi = lam_i * dt
    er = jnp.exp(xr)
    lbr = er * jnp.cos(xi)
    lbi = er * jnp.sin(xi)
    den = lam_r * lam_r + lam_i * lam_i
    cfr = ((lbr - 1.0) * lam_r + lbi * lam_i) / den
    cfi = (lbi * lam_r - (lbr - 1.0) * lam_i) / den
    bbr = cfr * bre_ref[0] - cfi * bim_ref[0]
    bbi = cfr * bim_ref[0] + cfi * bre_ref[0]
    kk = lax.broadcasted_iota(jnp.int32, (CHUNK, S5_N), 0).astype(F32)
    pe = jnp.exp(kk * xr)
    pwr = pe * jnp.cos(kk * xi)
    pwi = pe * jnp.sin(kk * xi)
    for s in range(CHUNK):
        pr = pwr[CHUNK - 1 - s:CHUNK - s, :]
        pi = pwi[CHUNK - 1 - s:CHUNK - s, :]
        wzr_ref[0, S5_P * s:S5_P * (s + 1), :] = (bbr * pr - bbi * pi).astype(wzr_ref.dtype)
        wzi_ref[0, S5_P * s:S5_P * (s + 1), :] = (bbr * pi + bbi * pr).astype(wzi_ref.dtype)
    e64 = jnp.exp(CHUNK * xr)
    ar_ref[0] = e64 * jnp.cos(CHUNK * xi)
    ai_ref[0] = e64 * jnp.sin(CHUNK * xi)
    xrc = lrc_ref[0] * dt
    xic = lic_ref[0] * dt
    tt = lax.broadcasted_iota(jnp.int32, (S5_N, CHUNK), 1).astype(F32)
    pte = jnp.exp(tt * xrc)
    ptr = pte * jnp.cos(tt * xic)
    pti = pte * jnp.sin(tt * xic)
    lane = lax.broadcasted_iota(jnp.int32, (CHUNK, S5_W), 1)
    rep_t = ((lane // S5_P) == lax.broadcasted_iota(jnp.int32, (CHUNK, S5_W), 0)).astype(F32)
    lane_p = lax.broadcasted_iota(jnp.int32, (S5_P, S5_W), 1)
    rep_p = ((lane_p % S5_P) == lax.broadcasted_iota(jnp.int32, (S5_P, S5_W), 0)).astype(F32)
    pr_rep = _dot(ptr, rep_t, precision=HI)
    pi_rep = _dot(pti, rep_t, precision=HI)
    cr_rep = _dot(cre_ref[0], rep_p, precision=HI)
    ci_rep = _dot(cim_ref[0], rep_p, precision=HI)
    q0r = cr_rep * pr_rep - ci_rep * pi_rep
    q0i = cr_rep * pi_rep + ci_rep * pr_rep
    erc = jnp.exp(xrc)
    lbrc = erc * jnp.cos(xic)
    lbic = erc * jnp.sin(xic)
    ptr_ref[0] = (q0r * lbrc - q0i * lbic).astype(ptr_ref.dtype)
    pti_ref[0] = (-(q0r * lbic + q0i * lbrc)).astype(pti_ref.dtype)
    kern = _dot(bbr, q0r, precision=HI) - _dot(bbi, q0i, precision=HI)
    lane_w = lax.broadcasted_iota(jnp.int32, (S5_P, S5_W), 1)
    for s in range(SUBCHUNK):
        blk = kern if s == 0 else jnp.where(lane_w >= S5_P * s, pltpu.roll(kern, S5_P * s, axis=1), 0.0)
        strip_ref[0, S5_P * s:S5_P * (s + 1), :] = blk.astype(strip_ref.dtype)


def _s5_gen(lam_re, lam_im, log_dt, b_re_t, b_im_t, c_re_t, c_im_t):
    g = lam_re.shape[0]
    row3 = lambda a: a.reshape(g, 1, -1)
    col3 = lambda a: a.reshape(g, -1, 1)
    blk = lambda s: pl.BlockSpec((1,) + s, lambda i: (i, 0, 0))
    return pl.pallas_call(
        _s5_gen_kernel,
        grid=(g,),
        in_specs=[blk((1, S5_N)), blk((1, S5_N)), blk((S5_N, 1)), blk((S5_N, 1)), blk((1, 1)),
                  blk((S5_P, S5_N)), blk((S5_P, S5_N)), blk((S5_N, S5_P)), blk((S5_N, S5_P))],
        out_specs=[blk((S5_TILE, S5_W)), blk((S5_W, S5_N)), blk((S5_W, S5_N)),
                   blk((S5_N, S5_W)), blk((S5_N, S5_W)), blk((1, S5_N)), blk((1, S5_N))],
        out_shape=[
            jax.ShapeDtypeStruct((g, S5_TILE, S5_W), BF16),
            jax.ShapeDtypeStruct((g, S5_W, S5_N), BF16),
            jax.ShapeDtypeStruct((g, S5_W, S5_N), BF16),
            jax.ShapeDtypeStruct((g, S5_N, S5_W), BF16),
            jax.ShapeDtypeStruct((g, S5_N, S5_W), BF16),
            jax.ShapeDtypeStruct((g, 1, S5_N), F32),
            jax.ShapeDtypeStruct((g, 1, S5_N), F32),
        ],
        compiler_params=_cparams("parallel"),
        name="s5_gen",
    )(row3(lam_re), row3(lam_im), col3(lam_re), col3(lam_im), log_dt.reshape(g, 1, 1),
      b_re_t, b_im_t, c_re_t, c_im_t)


def _s5_apply_kernel(u_ref, strip_ref, wzr_ref, wzi_ref, ptr_ref, pti_ref, ar_ref, ai_ref,
                     y_ref, zr_scr, zi_scr, xr_scr, xi_scr, *, bsz, nchunks):
    u = u_ref[0]
    zr_scr[...] = _dot(u, wzr_ref[0])
    zi_scr[...] = _dot(u, wzi_ref[0])
    a_r = ar_ref[0]
    a_i = ai_ref[0]

    def step(c, carry):
        x_r, x_i = carry
        rows = pl.ds(pl.multiple_of(c * bsz, bsz), bsz)
        xr_scr[rows, :] = x_r
        xi_scr[rows, :] = x_i
        n_r = a_r * x_r - a_i * x_i + zr_scr[rows, :]
        n_i = a_r * x_i + a_i * x_r + zi_scr[rows, :]
        return n_r, n_i

    zero = jnp.zeros((bsz, S5_N), F32)
    lax.fori_loop(0, nchunks, step, (zero, zero))
    xr = xr_scr[...].astype(BF16)
    xi = xi_scr[...].astype(BF16)
    nt = S5_W // S5_TILE
    for j in range(nt):
        cols = slice(S5_TILE * j, S5_TILE * (j + 1))
        acc = _dot(xr, ptr_ref[0, :, cols]) + _dot(xi, pti_ref[0, :, cols])
        for i in range(j + 1):
            acc = acc + _dot(u[:, S5_TILE * i:S5_TILE * (i + 1)],
                             strip_ref[0, :, S5_TILE * (j - i):S5_TILE * (j - i + 1)])
        y_ref[0, :, cols] = acc.astype(y_ref.dtype)


def _s5_apply(ut, strip, wzr, wzi, ptr, pti, ar, ai, *, bsz, nchunks):
    g, rows, _ = ut.shape
    blk = lambda s: pl.BlockSpec((1,) + s, lambda i: (i, 0, 0))
    return pl.pallas_call(
        functools.partial(_s5_apply_kernel, bsz=bsz, nchunks=nchunks),
        grid=(g,),
        in_specs=[blk((rows, S5_W)), blk((S5_TILE, S5_W)), blk((S5_W, S5_N)), blk((S5_W, S5_N)),
                  blk((S5_N, S5_W)), blk((S5_N, S5_W)), blk((1, S5_N)), blk((1, S5_N))],
        out_specs=blk((rows, S5_W)),
        out_shape=jax.ShapeDtypeStruct((g, rows, S5_W), BF16),
        scratch_shapes=[pltpu.VMEM((rows, S5_N), F32)] * 4,
        compiler_params=_cparams("parallel"),
        name="s5_apply",
    )(ut, strip, wzr, wzi, ptr, pti, ar, ai)


def _s5_glu_kernel(h_ref, y_ref, g_ref, d_ref, w_ref, b_ref, o_ref, yv_ref, yb_ref, *, tn):
    j = pl.program_id(1)

    @pl.when(j == 0)
    def _():
        g = g_ref[...]
        dv = d_ref[...]

        def body(b, carry):
            rs = pl.ds(pl.multiple_of(b * GLU_ROWS, GLU_ROWS), GLU_ROWS)
            y = y_ref[rs, :].astype(F32) + dv * _rms(h_ref[rs, :], g)
            y = 0.5 * y * (1.0 + jnp.tanh(math.sqrt(2.0 / math.pi) * (y + 0.044715 * (y * y * y))))
            yv_ref[rs, :] = y
            yb_ref[rs, :] = y.astype(BF16)
            return carry

        lax.fori_loop(0, h_ref.shape[0] // GLU_ROWS, body, 0)

    cols = pl.ds(pl.multiple_of(j * tn, tn), tn)
    z = _dot(yb_ref[...], w_ref[...]) + b_ref[...]
    o_ref[...] = h_ref[:, cols] + yv_ref[:, cols] * _sigmoid(z)


def _s5_glu(h, y, g, dvec, w_glu, o, b_glu, *, tm, tn):
    n, d = h.shape
    return pl.pallas_call(
        functools.partial(_s5_glu_kernel, tn=tn),
        grid=(n // tm, d // tn),
        in_specs=[
            pl.BlockSpec((tm, d), lambda i, j: (i, 0)),
            pl.BlockSpec((tm, d), lambda i, j: (i, 0)),
            pl.BlockSpec((1, d), lambda i, j: (0, 0)),
            pl.BlockSpec((1, d), lambda i, j: (0, 0)),
            pl.BlockSpec((None, d, tn), lambda i, j: (o, 0, j)),
            pl.BlockSpec((1, tn), lambda i, j: (0, j)),
        ],
        out_specs=pl.BlockSpec((tm, tn), lambda i, j: (i, j)),
        out_shape=jax.ShapeDtypeStruct((n, d), F32),
        scratch_shapes=[pltpu.VMEM((tm, d), F32), pltpu.VMEM((tm, d), BF16)],
        compiler_params=_cparams("parallel", "arbitrary"),
        name="s5_glu",
    )(h, y, g, dvec, w_glu, b_glu)


def _even_layer(h, g_mix, w_main, w_gate, e, m_conv_w, m_conv_b, m_b_igate, m_b_fgate, m_head_norm,
                g_w_gate, g_b_gate, g_head_norm, w_out, *, bsz, seq, tm, tb):
    n, d = h.shape
    pm, gates = _inproj(h, g_mix.reshape(1, d), w_main, w_gate, e, tm=tm, tn=1024)
    gates_t = gates[:, 0:2 * M_HEADS].reshape(bsz, seq // CHUNK, CHUNK, 2, M_HEADS).transpose(3, 0, 1, 4, 2)
    gates_t = gates_t.reshape(2, bsz, seq // CHUNK * (M_HEADS // 2), 2 * CHUNK)
    bias = jnp.repeat(jnp.stack([m_b_igate, m_b_fgate]).astype(F32), CHUNK, axis=-1)
    bias = bias.reshape(2, M_HEADS // 2, 2 * CHUNK)
    hm = _mlstm(pm, gates_t, m_conv_w, m_conv_b.reshape(1, -1), bias, m_head_norm.reshape(1, -1),
                bsz=bsz, seq=seq, tb=tb)
    wg_pad = jnp.zeros((GATE_LANES, G_HEADS * G_DK), BF16).at[16:16 + G_RANK].set(g_w_gate.astype(BF16))
    hg = _gla(pm, gates, wg_pad, g_b_gate.reshape(1, -1), g_head_norm.reshape(1, -1),
              bsz=bsz, seq=seq, tb=tb)
    return _outproj(hm, hg, w_out, e, h, tm=tm, tn=1024)


def _odd_layer(h, u, g_mix, lam_re, lam_im, log_dt, b_re, b_im, c_re, c_im, dvec, w_glu, o, b_glu,
               *, bsz, seq, tm):
    n, d = h.shape
    groups = d // S5_P
    nchunks = seq // CHUNK
    ops = _s5_gen(lam_re, lam_im, log_dt, jnp.swapaxes(b_re, 1, 2), jnp.swapaxes(b_im, 1, 2),
                  jnp.swapaxes(c_re, 1, 2), jnp.swapaxes(c_im, 1, 2))
    ut = u.reshape(bsz, nchunks, CHUNK, groups, S5_P).transpose(3, 1, 0, 2, 4)
    ut = ut.reshape(groups, nchunks * bsz, S5_W)
    yt = _s5_apply(ut, *ops, bsz=bsz, nchunks=nchunks)
    y = yt.reshape(groups, nchunks, bsz, CHUNK, S5_P).transpose(2, 1, 3, 0, 4).reshape(n, d)
    return _s5_glu(h, y, g_mix.reshape(1, d), dvec.reshape(1, d), w_glu, o, b_glu.reshape(1, d),
                   tm=tm, tn=1024)


def _regroup_kernel(w_ref, o_ref, *, segments):
    for src, dst, n in segments:
        if src is None:
            o_ref[:, dst:dst + n] = jnp.zeros((o_ref.shape[0], n), o_ref.dtype)
        else:
            o_ref[:, dst:dst + n] = w_ref[:, src:src + n].astype(o_ref.dtype)


def _regroup_cols(w, segments, width, dtype, *, rows):
    nl, r, c = w.shape
    return pl.pallas_call(
        functools.partial(_regroup_kernel, segments=segments),
        grid=(nl, r // rows),
        in_specs=[pl.BlockSpec((None, rows, c), lambda l, i: (l, i, 0))],
        out_specs=pl.BlockSpec((None, rows, width), lambda l, i: (l, i, 0)),
        out_shape=jax.ShapeDtypeStruct((nl, r, width), dtype),
        compiler_params=_cparams("parallel", "parallel"),
        name="regroup_cols",
    )(w)


def _tile_up_kernel(w_ref, o_ref, *, dff, tk):
    for k in range(o_ref.shape[0]):
        n = min(tk, dff - k * tk)
        for half, base in ((0, 0), (1, dff)):
            o_ref[k, half, :, 0:n] = w_ref[:, base + k * tk:base + k * tk + n].astype(o_ref.dtype)
            if n < tk:
                o_ref[k, half, :, n:tk] = jnp.zeros((o_ref.shape[2], tk - n), o_ref.dtype)


def _tile_up(w, tk, *, rows):
    nl, d, two_ff = w.shape
    dff = two_ff // 2
    nk = -(-dff // tk)
    return pl.pallas_call(
        functools.partial(_tile_up_kernel, dff=dff, tk=tk),
        grid=(nl, d // rows),
        in_specs=[pl.BlockSpec((None, rows, two_ff), lambda l, i: (l, i, 0))],
        out_specs=pl.BlockSpec((None, nk, 2, rows, tk), lambda l, i: (l, 0, 0, i, 0)),
        out_shape=jax.ShapeDtypeStruct((nl, nk, 2, d, tk), BF16),
        compiler_params=_cparams("parallel", "parallel"),
        name="tile_up",
    )(w)


def _prep_ffn(ffn_w_up, ffn_conv_w, ffn_conv_b, ffn_w_down, tk):
    dff = ffn_w_down.shape[1]
    pad = -dff % tk
    dffp = dff + pad

    def padded(a, dtype):
        lead = a.shape[:-1]
        gv = jnp.pad(a.reshape(lead + (2, dff)), [(0, 0)] * len(lead) + [(0, 0), (0, pad)])
        return gv.astype(dtype).reshape(lead + (2 * dffp,))

    wup = _tile_up(ffn_w_up, tk, rows=128)
    cw = padded(ffn_conv_w, F32)
    cb = padded(ffn_conv_b, F32)[:, None, :]
    wdn = jnp.pad(ffn_w_down, [(0, 0), (0, pad), (0, 0)]).astype(BF16)
    return wup, cw, cb, wdn


def kernel(x, norm_mix, norm_ffn, ffn_w_up, ffn_conv_w, ffn_conv_b, ffn_w_down, norm_final,
           w_in, m_conv_w, m_conv_b, m_b_igate, m_b_fgate, m_head_norm,
           g_w_gate, g_b_gate, g_head_norm, w_out,
           s5_lambda_re, s5_lambda_im, s5_log_dt, s5_b_re, s5_b_im, s5_c_re, s5_c_im,
           s5_d, s5_w_glu, s5_b_glu):
    return _forward(x, norm_mix, norm_ffn, ffn_w_up, ffn_conv_w, ffn_conv_b, ffn_w_down, norm_final,
                    w_in, m_conv_w, m_conv_b, m_b_igate, m_b_fgate, m_head_norm,
                    g_w_gate, g_b_gate, g_head_norm, w_out,
                    s5_lambda_re, s5_lambda_im, s5_log_dt, s5_b_re, s5_b_im, s5_c_re, s5_c_im,
                    s5_d, s5_w_glu, s5_b_glu, tm=512, tf=512, tb=512)


def _forward(x, norm_mix, norm_ffn, ffn_w_up, ffn_conv_w, ffn_conv_b, ffn_w_down, norm_final,
             w_in, m_conv_w, m_conv_b, m_b_igate, m_b_fgate, m_head_norm,
             g_w_gate, g_b_gate, g_head_norm, w_out,
             s5_lambda_re, s5_lambda_im, s5_log_dt, s5_b_re, s5_b_im, s5_c_re, s5_c_im,
             s5_d, s5_w_glu, s5_b_glu, *, tm, tf, tb):
    bsz, seq, d = x.shape
    depth = norm_mix.shape[0]
    n = bsz * seq
    tmd = min(2 * tm, n)
    h = x.reshape(n, d)

    wup, cw, cb, wdn = _prep_ffn(ffn_w_up, ffn_conv_w, ffn_conv_b, ffn_w_down, 512)
    c0 = 2 * M_HEADS * M_DK + 2 * M_HEADS * M_DV
    c1 = c0 + 2 * M_HEADS
    c2 = c1 + 2 * G_HEADS * G_DK + 2 * G_HEADS * G_DV
    ng = (c1 - c0) + G_RANK
    w_main = _regroup_cols(w_in, ((0, 0, c0), (c1, c0, c2 - c1)), c0 + c2 - c1, BF16, rows=256)
    w_gate = _regroup_cols(w_in, ((c0, 0, c1 - c0), (c2, c1 - c0, G_RANK), (None, ng, GATE_LANES - ng)),
                           GATE_LANES, BF16, rows=256)
    w_out_b = w_out.astype(BF16)
    w_glu_b = s5_w_glu.astype(BF16)

    for layer in range(depth):
        if layer % 2 == 0:
            e = layer // 2
            h = _even_layer(h, norm_mix[layer], w_main, w_gate, e, m_conv_w[e], m_conv_b[e],
                            m_b_igate[e], m_b_fgate[e], m_head_norm[e], g_w_gate[e], g_b_gate[e],
                            g_head_norm[e], w_out_b, bsz=bsz, seq=seq, tm=tmd, tb=tb)
        else:
            o = layer // 2
            h = _odd_layer(h, u, norm_mix[layer], s5_lambda_re[o], s5_lambda_im[o], s5_log_dt[o],
                           s5_b_re[o], s5_b_im[o], s5_c_re[o], s5_c_im[o], s5_d[o], w_glu_b, o,
                           s5_b_glu[o], bsz=bsz, seq=seq, tm=tm)
        if layer == depth - 1:
            post, gpost = "inplace", norm_final
        elif layer % 2 == 0:
            post, gpost = "extra", norm_mix[layer + 1]
        else:
            post, gpost = None, norm_ffn[layer]
        outs = _ffn(h, norm_ffn[layer].reshape(1, d), wup, cw, cb, wdn, layer, gpost.reshape(1, d), post,
                    seq=seq, tm=tf, tk=512, rb=min(256, tf))
        h = outs[0]
        u = outs[1] if post == "extra" else None
    return h.reshape(bsz, seq, d)
```

```python
import functools
import math

import jax
import jax.numpy as jnp
from jax import lax
from jax.experimental import pallas as pl
from jax.experimental.pallas import tpu as pltpu

F32 = jnp.float32
BF16 = jnp.bfloat16
HI = lax.Precision.HIGHEST

EPS = 1e-6
CHUNK = 64
M_HEADS, M_DK, M_DV, M_CONV = 8, 64, 128, 4
G_HEADS, G_DK, G_DV, G_RANK, G_TAU = 4, 128, 256, 16, 16.0
S5_P, S5_N = 16, 64
FFN_CONV = 3
GATE_LANES = 128
SUBCHUNK = 16

VMEM_LIMIT = 56 * 1024 * 1024


def _cparams(*sem):
    return pltpu.CompilerParams(dimension_semantics=sem, vmem_limit_bytes=VMEM_LIMIT)


def _rms(x, g):
    return x * lax.rsqrt(jnp.mean(x * x, axis=-1, keepdims=True) + EPS) * g


def _sigmoid(x):
    return 0.5 + 0.5 * jnp.tanh(0.5 * x)


def _log_sigmoid(x):
    return jnp.minimum(x, 0.0) - jnp.log(1.0 + jnp.exp(-jnp.abs(x)))


def _dot(a, b, **kw):
    return jnp.dot(a, b, preferred_element_type=F32, **kw)


def _dot_nt(a, b, **kw):
    return lax.dot_general(a, b, (((1,), (1,)), ((), ())), preferred_element_type=F32, **kw)


def _dot_tn(a, b, **kw):
    return lax.dot_general(a, b, (((0,), (0,)), ((), ())), preferred_element_type=F32, **kw)


def _inproj_kernel(x_ref, g_ref, w_ref, wg_ref, o_ref, og_ref, xn_ref):
    @pl.when(pl.program_id(1) == 0)
    def _():
        _rms_rows_to(xn_ref, 0, x_ref, g_ref, x_ref.shape[0])
        og_ref[...] = _dot(xn_ref[...], wg_ref[...])

    o_ref[...] = _dot(xn_ref[...], w_ref[...]).astype(o_ref.dtype)


def _inproj(h, g, w_main, w_gate, e, *, tm, tn):
    n, d = h.shape
    wn = w_main.shape[2]
    return pl.pallas_call(
        _inproj_kernel,
        grid=(n // tm, wn // tn),
        in_specs=[
            pl.BlockSpec((tm, d), lambda i, j: (i, 0)),
            pl.BlockSpec((1, d), lambda i, j: (0, 0)),
            pl.BlockSpec((None, d, tn), lambda i, j: (e, 0, j)),
            pl.BlockSpec((None, d, GATE_LANES), lambda i, j: (e, 0, 0)),
        ],
        out_specs=[
            pl.BlockSpec((tm, tn), lambda i, j: (i, j)),
            pl.BlockSpec((tm, GATE_LANES), lambda i, j: (i, 0)),
        ],
        out_shape=[
            jax.ShapeDtypeStruct((n, wn), BF16),
            jax.ShapeDtypeStruct((n, GATE_LANES), F32),
        ],
        scratch_shapes=[pltpu.VMEM((tm, d), BF16)],
        compiler_params=_cparams("parallel", "arbitrary"),
        name="inproj",
    )(h, g, w_main, w_gate)


def _mlstm_select():
    npair = M_HEADS // 2
    npc = 2 * npair * 2 * CHUNK
    jrow = lax.broadcasted_iota(jnp.int32, (96, npc + M_HEADS * M_DV), 0)
    ncol = lax.broadcasted_iota(jnp.int32, (96, npc + M_HEADS * M_DV), 1)
    half = jrow // 48
    j16 = jrow % 16
    sel_pair = (ncol < npc) & (ncol // (2 * CHUNK) == j16) & ((ncol % (2 * CHUNK)) // CHUNK == half) & (j16 < 8)
    sel_em = (ncol >= npc) & (j16 >= 8) & (j16 < 12) & ((ncol - npc) // M_DV == 2 * (j16 - 8) + half)
    return (sel_pair | sel_em).astype(F32)


def _mlstm_kernel(mq_ref, mk_ref, mv_ref, mo_ref, li_ref, lf_ref, cw_ref, cb_ref, bi_ref, bf_ref, hn_ref,
                  sel_ref, o_ref, qk_scr, qkc_scr, c_scr, m_scr, row_scr, rep_scr, *, tb):
    @pl.when(pl.program_id(1) == 0)
    def _():
        qk_scr[0:8, :] = jnp.zeros((8, 2 * M_HEADS * M_DK), F32)
        c_scr[...] = jnp.zeros(c_scr.shape, F32)
        m_scr[...] = jnp.zeros(m_scr.shape, F32)

    nqk = M_HEADS * M_DK
    qk_scr[8:8 + tb, 0:nqk] = mq_ref[...].astype(F32)
    qk_scr[8:8 + tb, nqk:2 * nqk] = mk_ref[...].astype(F32)
    conv = cb_ref[...] + cw_ref[0:1, :] * qk_scr[5:5 + tb, :]
    for j in range(1, M_CONV):
        conv = conv + cw_ref[j:j + 1, :] * qk_scr[5 + j:5 + j + tb, :]
    tail = qk_scr[tb:tb + 8, :]
    qkc_scr[...] = conv * _sigmoid(conv)
    qk_scr[0:8, :] = tail

    npair = M_HEADS // 2
    lane = lax.broadcasted_iota(jnp.int32, (1, 2 * CHUNK), 1)
    lo_half = lane < CHUNK
    pos = lane % CHUNK
    trow = lax.broadcasted_iota(jnp.int32, (CHUNK, 2 * CHUNK), 0)
    causal2 = (lax.broadcasted_iota(jnp.int32, (CHUNK, 2 * CHUNK), 1) % CHUNK) <= trow
    r2 = lax.broadcasted_iota(jnp.int32, (2 * CHUNK, 2 * CHUNK), 0)
    c2 = lax.broadcasted_iota(jnp.int32, (2 * CHUNK, 2 * CHUNK), 1)
    tri2 = ((r2 // CHUNK == c2 // CHUNK) & (r2 <= c2)).astype(F32)
    row_dk = lax.broadcasted_iota(jnp.int32, (2 * M_DK, 1), 0)
    row_lo = row_dk < M_DK
    ones_v = jnp.ones((CHUNK, M_DV), BF16)
    zeros_v = jnp.zeros((CHUNK, 2 * M_DV), BF16)
    ones_sum = jnp.ones((M_DV, M_DV), BF16)
    ones_n = jnp.ones((CHUNK, M_DV), F32)
    scale = M_DK ** -0.5
    sel = sel_ref[...]

    def exact3(x):
        hi = x.astype(BF16).astype(F32)
        mid = (x - hi).astype(BF16).astype(F32)
        return hi, mid, x - hi - mid

    def half_max(x):
        m0 = jnp.max(jnp.where(lo_half, x, -jnp.inf), axis=-1, keepdims=True)
        m1 = jnp.max(jnp.where(lo_half, -jnp.inf, x), axis=-1, keepdims=True)
        return jnp.where(lo_half, m0, m1)

    nchunk = tb // CHUNK
    li = li_ref[0] + bi_ref[...]
    b = _dot(_log_sigmoid(lf_ref[0] + bf_ref[...]), tri2, precision=HI)
    a = li - b
    cm = a
    for sh in (1, 2, 4, 8, 16, 32):
        cm = jnp.maximum(cm, jnp.where(pos >= sh, pltpu.roll(cm, sh, 1), -jnp.inf))
    b_last = jnp.where(lo_half, b[:, CHUNK - 1:CHUNK], b[:, 2 * CHUNK - 1:2 * CHUNK])
    g = b_last + a
    gmax = half_max(g)
    m = m_scr[...]
    m_starts = []
    for c in range(nchunk):
        m_starts.append(m)
        m = jnp.maximum(b_last[npair * c:npair * (c + 1)] + m, gmax[npair * c:npair * (c + 1)])
    m_scr[...] = m
    m_prev = jnp.concatenate(m_starts, axis=0)
    m_new = jnp.maximum(b_last + m_prev, gmax)
    m_out = b + jnp.maximum(m_prev, cm)
    e1 = b - m_out
    terms = exact3(e1) + exact3(jnp.exp(e1 + m_prev)) + exact3(jnp.exp(-m_out))
    wk = jnp.exp(g - m_new)
    decay = jnp.exp(b_last + m_prev - m_new)
    zero4 = jnp.zeros((npair, 2 * CHUNK), F32)
    for c in range(nchunk):
        rs = slice(npair * c, npair * (c + 1))
        row_scr[c, 0:npair] = a[rs]
        row_scr[c, npair:2 * npair] = wk[rs]
        row_scr[c, 2 * npair:3 * npair] = decay[rs]
        q48 = jnp.concatenate([x for t in range(3) for x in (terms[t][rs], terms[3 + t][rs], terms[6 + t][rs], zero4)],
                              axis=0)
        qfull = jnp.concatenate([q48[:, 0:CHUNK], q48[:, CHUNK:2 * CHUNK]], axis=0)
        rep_scr[c] = _dot_tn(qfull, sel)

    def chunk(c, carry):
        rows = pl.ds(pl.multiple_of(c * CHUNK, CHUNK), CHUNK)
        a = row_scr[c, 0:npair]
        wk = row_scr[c, npair:2 * npair]
        decay = row_scr[c, 2 * npair:3 * npair]
        rep = rep_scr[c]
        for p in range(npair):
            qp = qkc_scr[rows, 2 * M_DK * p:2 * M_DK * (p + 1)] * scale
            kt = qkc_scr[rows, nqk + 2 * M_DK * p:nqk + 2 * M_DK * (p + 1)].T
            kt2 = jnp.concatenate([jnp.where(row_lo, kt, 0.0), jnp.where(row_lo, 0.0, kt)], axis=1)
            e1c = rep[:, 2 * CHUNK * p:2 * CHUNK * (p + 1)]
            wic = rep[:, 2 * CHUNK * (npair + p):2 * CHUNK * (npair + p + 1)]
            dexp = jnp.exp(jnp.where(causal2, a[p:p + 1, :] + e1c, -jnp.inf))
            smat = (_dot(qp.astype(BF16), kt2.astype(BF16)) * dexp).astype(BF16)
            v0 = mv_ref[rows, M_DV * 2 * p:M_DV * (2 * p + 1)]
            v1 = mv_ref[rows, M_DV * (2 * p + 1):M_DV * (2 * p + 2)]
            vbd = jnp.concatenate([jnp.concatenate([v0, ones_v, zeros_v], axis=1),
                                   jnp.concatenate([zeros_v, v1, ones_v], axis=1)], axis=0)
            cbd = c_scr[p]
            numext = _dot(smat, vbd) + _dot((qp * wic).astype(BF16), cbd.astype(BF16))
            for e in range(2):
                h = 2 * p + e
                num = numext[:, 2 * M_DV * e:2 * M_DV * e + M_DV]
                den = numext[:, 2 * M_DV * e + M_DV:2 * M_DV * (e + 1)]
                emc = rep[:, 2 * npair * 2 * CHUNK + M_DV * h:2 * npair * 2 * CHUNK + M_DV * (h + 1)]
                hh = num / jnp.maximum(jnp.abs(den), emc)
                sq = hh * hh
                sq_hi = sq.astype(BF16)
                ms = (_dot(sq_hi, ones_sum) + _dot((sq - sq_hi.astype(F32)).astype(BF16), ones_sum)) * (1.0 / M_DV)
                y = hh * lax.rsqrt(ms + EPS) * hn_ref[:, M_DV * h:M_DV * (h + 1)]
                og = _sigmoid(mo_ref[rows, M_DV * h:M_DV * (h + 1)].astype(F32))
                o_ref[rows, M_DV * h:M_DV * (h + 1)] = (og * y).astype(o_ref.dtype)
            wkp = wk[p:p + 1, :]
            kts = jnp.where(row_lo, kt * wkp[:, 0:CHUNK], kt * wkp[:, CHUNK:2 * CHUNK])
            kts = kts.astype(BF16).astype(F32)
            uc = _dot(kts, jnp.concatenate([v0, v1], axis=1).astype(F32))
            un = _dot(kts, ones_n)
            zc = jnp.zeros((2 * M_DK, M_DV), F32)
            upd = jnp.where(row_lo, jnp.concatenate([uc[:, 0:M_DV], un, zc, zc], axis=1),
                            jnp.concatenate([zc, zc, uc[:, M_DV:2 * M_DV], un], axis=1))
            dp = decay[p:p + 1, :]
            dcol = jnp.where(row_lo, dp[:, 0:1], dp[:, CHUNK:CHUNK + 1])
            c_scr[p] = dcol * cbd + upd
        return carry

    lax.fori_loop(0, tb // CHUNK, chunk, 0, unroll=4)


def _mlstm(pm, gates_t, conv_w, conv_b, bias, hnorm, *, bsz, seq, tb):
    n = pm.shape[0]
    nt = seq // tb
    nqk = M_HEADS * M_DK
    nv = M_HEADS * M_DV
    sel = _mlstm_select()
    rpb = tb // CHUNK * (M_HEADS // 2)
    bias_i = jnp.tile(bias[0], (tb // CHUNK, 1))
    bias_f = jnp.tile(bias[1], (tb // CHUNK, 1))
    return pl.pallas_call(
        functools.partial(_mlstm_kernel, tb=tb),
        grid=(bsz, nt),
        in_specs=[
            pl.BlockSpec((tb, nqk), lambda b, t: (b * nt + t, 0)),
            pl.BlockSpec((tb, nqk), lambda b, t: (b * nt + t, 1)),
            pl.BlockSpec((tb, nv), lambda b, t: (b * nt + t, 1)),
            pl.BlockSpec((tb, nv), lambda b, t: (b * nt + t, 2)),
            pl.BlockSpec((None, 1, rpb, 2 * CHUNK), lambda b, t: (0, b, t, 0)),
            pl.BlockSpec((None, 1, rpb, 2 * CHUNK), lambda b, t: (1, b, t, 0)),
            pl.BlockSpec((M_CONV, 2 * nqk), lambda b, t: (0, 0)),
            pl.BlockSpec((1, 2 * nqk), lambda b, t: (0, 0)),
            pl.BlockSpec((rpb, 2 * CHUNK), lambda b, t: (0, 0)),
            pl.BlockSpec((rpb, 2 * CHUNK), lambda b, t: (0, 0)),
            pl.BlockSpec((1, nv), lambda b, t: (0, 0)),
            pl.BlockSpec(sel.shape, lambda b, t: (0, 0)),
        ],
        out_specs=pl.BlockSpec((tb, nv), lambda b, t: (b * nt + t, 0)),
        out_shape=jax.ShapeDtypeStruct((n, nv), BF16),
        scratch_shapes=[
            pltpu.VMEM((tb + 8, 2 * nqk), F32),
            pltpu.VMEM((tb, 2 * nqk), F32),
            pltpu.VMEM((M_HEADS // 2, 2 * M_DK, 4 * M_DV), F32),
            pltpu.VMEM((M_HEADS // 2, 2 * CHUNK), F32),
            pltpu.VMEM((tb // CHUNK, 2 * M_HEADS, 2 * CHUNK), F32),
            pltpu.VMEM((tb // CHUNK, CHUNK, sel.shape[1]), F32),
        ],
        compiler_params=_cparams("parallel", "arbitrary"),
        name="mlstm",
    )(pm, pm, pm, pm, gates_t, gates_t, conv_w, conv_b, bias_i, bias_f, hnorm, sel)


def _gla_kernel(gq_ref, gk_ref, gv_ref, gg_ref, gc_ref, wg_ref, bg_ref, hn_ref, o_ref, s_scr, bc_scr, *, tb):
    @pl.when(pl.program_id(1) == 0)
    def _():
        s_scr[...] = jnp.zeros(s_scr.shape, F32)

    row = lax.broadcasted_iota(jnp.int32, (CHUNK, CHUNK), 0)
    col = lax.broadcasted_iota(jnp.int32, (CHUNK, CHUNK), 1)
    tril = row >= col
    tri = tril.astype(F32)
    rowk = lax.broadcasted_iota(jnp.int32, (CHUNK, 1), 0)
    nsub = CHUNK // SUBCHUNK
    scale = G_DK ** -0.5

    pre = _dot(gc_ref[...].astype(BF16), wg_ref[...]) + bg_ref[...]
    la = _log_sigmoid(pre) * (1.0 / G_TAU)
    for c in range(tb // CHUNK):
        bc_scr[CHUNK * c:CHUNK * (c + 1), :] = _dot(tri, la[CHUNK * c:CHUNK * (c + 1), :], precision=HI)

    def chunk(c, carry):
        r0 = pl.multiple_of(c * CHUNK, CHUNK)
        rows = pl.ds(r0, CHUNK)
        bc_all = bc_scr[rows, :]
        for h in range(G_HEADS):
            ks = slice(G_DK * h, G_DK * (h + 1))
            vs = slice(G_DV * h, G_DV * (h + 1))
            bc = bc_all[:, ks]
            q = gq_ref[rows, ks].astype(F32) * scale
            k = gk_ref[rows, ks].astype(F32)
            v = gv_ref[rows, vs]
            st = s_scr[h]
            o = _dot_nt((q * jnp.exp(bc)).astype(BF16), st.astype(BF16))
            cblk = jnp.concatenate(
                [jnp.broadcast_to(bc[SUBCHUNK * i:SUBCHUNK * i + 1, :], (SUBCHUNK, G_DK))
                 for i in range(nsub)], axis=0)
            qt = (q * jnp.exp(bc - cblk)).astype(BF16)
            blocks = []
            for i in range(nsub):
                ci = bc[SUBCHUNK * i:SUBCHUNK * i + 1, :]
                kt = jnp.where(rowk < SUBCHUNK * (i + 1), k * jnp.exp(ci - bc), 0.0).astype(BF16)
                blocks.append(_dot_nt(qt[SUBCHUNK * i:SUBCHUNK * (i + 1), :], kt))
            a = jnp.where(tril, jnp.concatenate(blocks, axis=0), 0.0)
            o = o + _dot(a.astype(BF16), v)
            y = o * lax.rsqrt(jnp.mean(o * o, axis=-1, keepdims=True) + EPS) * hn_ref[:, vs]
            gg = gg_ref[rows, vs].astype(F32)
            o_ref[rows, vs] = (gg * _sigmoid(gg) * y).astype(o_ref.dtype)
            last = bc[CHUNK - 1:CHUNK, :]
            kd = (k * jnp.exp(last - bc)).astype(BF16)
            s_scr[h] = st * jnp.exp(last) + _dot_tn(v, kd)
        return carry

    lax.fori_loop(0, tb // CHUNK, chunk, 0, unroll=4)


def _gla(pm, gates, wg_pad, bg, hnorm, *, bsz, seq, tb):
    n = pm.shape[0]
    nt = seq // tb
    nqk = G_HEADS * G_DK
    nv = G_HEADS * G_DV
    rowmap = lambda b, t: (b * nt + t, 0)
    return pl.pallas_call(
        functools.partial(_gla_kernel, tb=tb),
        grid=(bsz, nt),
        in_specs=[
            pl.BlockSpec((tb, nqk), lambda b, t: (b * nt + t, 6)),
            pl.BlockSpec((tb, nqk), lambda b, t: (b * nt + t, 7)),
            pl.BlockSpec((tb, nv), lambda b, t: (b * nt + t, 4)),
            pl.BlockSpec((tb, nv), lambda b, t: (b * nt + t, 5)),
            pl.BlockSpec((tb, GATE_LANES), rowmap),
            pl.BlockSpec((GATE_LANES, nqk), lambda b, t: (0, 0)),
            pl.BlockSpec((1, nqk), lambda b, t: (0, 0)),
            pl.BlockSpec((1, nv), lambda b, t: (0, 0)),
        ],
        out_specs=pl.BlockSpec((tb, nv), rowmap),
        out_shape=jax.ShapeDtypeStruct((n, nv), BF16),
        scratch_shapes=[pltpu.VMEM((G_HEADS, G_DV, G_DK), F32), pltpu.VMEM((tb, nqk), F32)],
        compiler_params=_cparams("parallel", "arbitrary"),
        name="gla",
    )(pm, pm, pm, pm, gates, wg_pad, bg, hnorm)


def _outproj_kernel(hm_ref, hg_ref, w1_ref, w2_ref, h_ref, o_ref):
    o_ref[...] = h_ref[...] + _dot(hm_ref[...], w1_ref[...]) + _dot(hg_ref[...], w2_ref[...])


def _outproj(hm, hg, w_out, e, h, *, tm, tn):
    n, d = h.shape
    kh = hm.shape[1]
    return pl.pallas_call(
        _outproj_kernel,
        grid=(n // tm, d // tn),
        in_specs=[
            pl.BlockSpec((tm, kh), lambda i, j: (i, 0)),
            pl.BlockSpec((tm, kh), lambda i, j: (i, 0)),
            pl.BlockSpec((None, kh, tn), lambda i, j: (e, 0, j)),
            pl.BlockSpec((None, kh, tn), lambda i, j: (e, 1, j)),
            pl.BlockSpec((tm, tn), lambda i, j: (i, j)),
        ],
        out_specs=pl.BlockSpec((tm, tn), lambda i, j: (i, j)),
        out_shape=jax.ShapeDtypeStruct((n, d), F32),
        compiler_params=_cparams("parallel", "arbitrary"),
        name="outproj",
    )(hm, hg, w_out, w_out, h)


FFN_HALO = 16
MXU_COLS = 256
NORM_ROWS = 128
GLU_ROWS = 64


def _rms_rows_to(dst_ref, dst_off, src_ref, g_ref, rows):
    g = g_ref[...]
    for r0 in range(0, rows, NORM_ROWS):
        nr = min(NORM_ROWS, rows - r0)
        dst_ref[dst_off + r0:dst_off + r0 + nr, :] = _rms(src_ref[r0:r0 + nr, :], g).astype(dst_ref.dtype)


def _ffn_kernel(h_ref, halo_ref, g_ref, wg_ref, wv_ref, cwg_ref, cwv_ref, cbg_ref, cbv_ref, wdn_ref,
                gpost_ref, *rest, tm, tk, rb, tiles_per_seq, post):
    if post == "extra":
        o_ref, u_ref, xn_ref, a_ref = rest
    else:
        o_ref, xn_ref, a_ref = rest
    i = pl.program_id(0)

    @pl.when(pl.program_id(1) == 0)
    def _():
        _rms_rows_to(xn_ref, 0, halo_ref, g_ref, FFN_HALO)
        _rms_rows_to(xn_ref, FFN_HALO, h_ref, g_ref, tm)
        o_ref[...] = h_ref[...]

    keep = ((i % tiles_per_seq) != 0).astype(F32)

    def up(r0, nr):
        for c0 in range(0, tk, MXU_COLS):
            a_ref[r0:r0 + nr, c0:c0 + MXU_COLS] = _dot(xn_ref[r0:r0 + nr, :], wg_ref[:, c0:c0 + MXU_COLS])
            a_ref[r0:r0 + nr, tk + c0:tk + c0 + MXU_COLS] = _dot(xn_ref[r0:r0 + nr, :], wv_ref[:, c0:c0 + MXU_COLS])

    up(0, FFN_HALO + rb)
    a_ref[0:FFN_HALO, :] = a_ref[0:FFN_HALO, :] * keep
    for r in range(1, tm // rb):
        up(FFN_HALO + r * rb, rb)
    w0 = jnp.concatenate([cwg_ref[0:1, :], cwv_ref[0:1, :]], axis=-1)
    w1 = jnp.concatenate([cwg_ref[1:2, :], cwv_ref[1:2, :]], axis=-1)
    w2 = jnp.concatenate([cwg_ref[2:3, :], cwv_ref[2:3, :]], axis=-1)
    cb = jnp.concatenate([cbg_ref[...], cbv_ref[...]], axis=-1)
    for r in range(tm // rb):
        r0 = FFN_HALO + r * rb
        c = (w2 * a_ref[r0:r0 + rb, :] + w1 * a_ref[r0 - 1:r0 - 1 + rb, :]
             + w0 * a_ref[r0 - 2:r0 - 2 + rb, :] + cb)
        hg = 0.5 * c[:, 0:tk]
        act = ((hg + hg * jnp.tanh(hg)) * c[:, tk:2 * tk]).astype(BF16)
        o_ref[r * rb:(r + 1) * rb, :] += _dot(act, wdn_ref[...])

    if post is not None:
        @pl.when(pl.program_id(1) == pl.num_programs(1) - 1)
        def _():
            _rms_rows_to(u_ref if post == "extra" else o_ref, 0, o_ref, gpost_ref, tm)


def _ffn(h, g, wup, cw, cb, wdn, layer, gpost, post, *, seq, tm, tk, rb):
    n, d = h.shape
    nk = wdn.shape[1] // tk
    hb = tm // FFN_HALO
    out_specs = [pl.BlockSpec((tm, d), lambda i, k: (i, 0))]
    out_shape = [jax.ShapeDtypeStruct((n, d), F32)]
    if post == "extra":
        out_specs.append(pl.BlockSpec((tm, d), lambda i, k: (i, 0)))
        out_shape.append(jax.ShapeDtypeStruct((n, d), BF16))
    return pl.pallas_call(
        functools.partial(_ffn_kernel, tm=tm, tk=tk, rb=rb, tiles_per_seq=seq // tm, post=post),
        grid=(n // tm, nk),
        in_specs=[
            pl.BlockSpec((tm, d), lambda i, k: (i, 0)),
            pl.BlockSpec((FFN_HALO, d), lambda i, k: (jnp.maximum(i * hb - 1, 0), 0)),
            pl.BlockSpec((1, d), lambda i, k: (0, 0)),
            pl.BlockSpec((None, None, None, d, tk), lambda i, k: (layer, k, 0, 0, 0)),
            pl.BlockSpec((None, None, None, d, tk), lambda i, k: (layer, k, 1, 0, 0)),
            pl.BlockSpec((None, FFN_CONV, tk), lambda i, k: (layer, 0, k)),
            pl.BlockSpec((None, FFN_CONV, tk), lambda i, k: (layer, 0, nk + k)),
            pl.BlockSpec((None, 1, tk), lambda i, k: (layer, 0, k)),
            pl.BlockSpec((None, 1, tk), lambda i, k: (layer, 0, nk + k)),
            pl.BlockSpec((None, tk, d), lambda i, k: (layer, k, 0)),
            pl.BlockSpec((1, d), lambda i, k: (0, 0)),
        ],
        out_specs=out_specs,
        out_shape=out_shape,
        scratch_shapes=[
            pltpu.VMEM((tm + FFN_HALO, d), BF16),
            pltpu.VMEM((tm + FFN_HALO, 2 * tk), F32),
        ],
        compiler_params=_cparams("parallel", "arbitrary"),
        name="convffn",
    )(h, h, g, wup, wup, cw, cw, cb, cb, wdn, gpost)


S5_TILE = SUBCHUNK * S5_P
S5_W = CHUNK * S5_P


def _s5_gen_kernel(lr_ref, li_ref, dt_ref, bre_ref, bim_ref, cre_ref, cim_ref,
                   strip_ref, wz_ref, pt_ref, ar_ref, ai_ref):
    dt = jnp.exp(dt_ref[0])
    lam_r = lr_ref[0]
    lam_i = li_ref[0]
    xr = lam_r * dt
    xi = lam_i * dt
    er = jnp.exp(xr)
    lbr = er * jnp.cos(xi)
    lbi = er * jnp.sin(xi)
    den = lam_r * lam_r + lam_i * lam_i
    cfr = ((lbr - 1.0) * lam_r + lbi * lam_i) / den
    cfi = (lbi * lam_r - (lbr - 1.0) * lam_i) / den
    bbr = cfr * bre_ref[0] - cfi * bim_ref[0]
    bbi = cfr * bim_ref[0] + cfi * bre_ref[0]
    kk = lax.broadcasted_iota(jnp.int32, (CHUNK, S5_N), 0).astype(F32)
    pe = jnp.exp(kk * xr)
    pwr = pe * jnp.cos(kk * xi)
    pwi = pe * jnp.sin(kk * xi)
    for s in range(CHUNK):
        pr = pwr[CHUNK - 1 - s:CHUNK - s, :]
        pi = pwi[CHUNK - 1 - s:CHUNK - s, :]
        wz_ref[0, S5_P * s:S5_P * (s + 1), 0:S5_N] = (bbr * pr - bbi * pi).astype(wz_ref.dtype)
        wz_ref[0, S5_P * s:S5_P * (s + 1), S5_N:2 * S5_N] = (bbr * pi + bbi * pr).astype(wz_ref.dtype)
    e64 = jnp.exp(CHUNK * xr)
    ar_ref[0] = e64 * jnp.cos(CHUNK * xi)
    ai_ref[0] = e64 * jnp.sin(CHUNK * xi)
    ptr = pwr.T
    pti = pwi.T
    lane = lax.broadcasted_iota(jnp.int32, (CHUNK, S5_W), 1)
    rep_t = ((lane // S5_P) == lax.broadcasted_iota(jnp.int32, (CHUNK, S5_W), 0)).astype(F32)
    lane_p = lax.broadcasted_iota(jnp.int32, (S5_P, S5_W), 1)
    rep_p = ((lane_p % S5_P) == lax.broadcasted_iota(jnp.int32, (S5_P, S5_W), 0)).astype(F32)
    pr_rep = _dot(ptr, rep_t, precision=HI)
    pi_rep = _dot(pti, rep_t, precision=HI)
    cr_rep = _dot(cre_ref[0], rep_p, precision=HI)
    ci_rep = _dot(cim_ref[0], rep_p, precision=HI)
    q0r = cr_rep * pr_rep - ci_rep * pi_rep
    q0i = cr_rep * pi_rep + ci_rep * pr_rep
    lbrc = ptr[:, 1:2]
    lbic = pti[:, 1:2]
    pt_ref[0, 0:S5_N, :] = (q0r * lbrc - q0i * lbic).astype(pt_ref.dtype)
    pt_ref[0, S5_N:2 * S5_N, :] = (-(q0r * lbic + q0i * lbrc)).astype(pt_ref.dtype)
    kern = _dot(bbr, q0r, precision=HI) - _dot(bbi, q0i, precision=HI)
    lane_w = lax.broadcasted_iota(jnp.int32, (S5_P, S5_W), 1)
    for s in range(SUBCHUNK):
        blk = kern if s == 0 else jnp.where(lane_w >= S5_P * s, pltpu.roll(kern, S5_P * s, axis=1), 0.0)
        strip_ref[0, S5_P * s:S5_P * (s + 1), :] = blk.astype(strip_ref.dtype)


def _s5_gen(lam_re, lam_im, log_dt, b_re_t, b_im_t, c_re_t, c_im_t):
    g = lam_re.shape[0]
    row3 = lambda a: a.reshape(g, 1, -1)
    blk = lambda s: pl.BlockSpec((1,) + s, lambda i: (i, 0, 0))
    return pl.pallas_call(
        _s5_gen_kernel,
        grid=(g,),
        in_specs=[blk((1, S5_N)), blk((1, S5_N)), blk((1, 1)),
                  blk((S5_P, S5_N)), blk((S5_P, S5_N)), blk((S5_N, S5_P)), blk((S5_N, S5_P))],
        out_specs=[blk((S5_TILE, S5_W)), blk((S5_W, 2 * S5_N)), blk((2 * S5_N, S5_W)),
                   blk((1, S5_N)), blk((1, S5_N))],
        out_shape=[
            jax.ShapeDtypeStruct((g, S5_TILE, S5_W), BF16),
            jax.ShapeDtypeStruct((g, S5_W, 2 * S5_N), BF16),
            jax.ShapeDtypeStruct((g, 2 * S5_N, S5_W), BF16),
            jax.ShapeDtypeStruct((g, 1, S5_N), F32),
            jax.ShapeDtypeStruct((g, 1, S5_N), F32),
        ],
        compiler_params=_cparams("parallel"),
        name="s5_gen",
    )(row3(lam_re), row3(lam_im), log_dt.reshape(g, 1, 1), b_re_t, b_im_t, c_re_t, c_im_t)


def _s5_apply_kernel(u_ref, strip_ref, wz_ref, pt_ref, ar_ref, ai_ref,
                     y_ref, zr_scr, zi_scr, xr_scr, xi_scr, acc_scr, *, bsz, nchunks):
    u = u_ref[0]
    z = _dot(u, wz_ref[0])
    zr_scr[...] = z[:, 0:S5_N]
    zi_scr[...] = z[:, S5_N:2 * S5_N]
    nt = S5_W // S5_TILE
    for j in range(nt):
        acc = _dot(u[:, 0:S5_TILE], strip_ref[0, :, S5_TILE * j:S5_TILE * (j + 1)])
        for i in range(1, j + 1):
            acc = acc + _dot(u[:, S5_TILE * i:S5_TILE * (i + 1)],
                             strip_ref[0, :, S5_TILE * (j - i):S5_TILE * (j - i + 1)])
        acc_scr[:, S5_TILE * j:S5_TILE * (j + 1)] = acc
    a_r = ar_ref[0]
    a_i = ai_ref[0]

    def step(c, carry):
        x_r, x_i = carry
        rows = pl.ds(pl.multiple_of(c * bsz, bsz), bsz)
        xr_scr[rows, :] = x_r
        xi_scr[rows, :] = x_i
        n_r = a_r * x_r - a_i * x_i + zr_scr[rows, :]
        n_i = a_r * x_i + a_i * x_r + zi_scr[rows, :]
        return n_r, n_i

    zero = jnp.zeros((bsz, S5_N), F32)
    lax.fori_loop(0, nchunks, step, (zero, zero), unroll=8)
    xcat = jnp.concatenate([xr_scr[...], xi_scr[...]], axis=1).astype(BF16)
    for j in range(nt):
        cols = slice(S5_TILE * j, S5_TILE * (j + 1))
        y_ref[0, :, cols] = (acc_scr[:, cols] + _dot(xcat, pt_ref[0, :, cols])).astype(y_ref.dtype)


def _s5_apply(ut, strip, wz, pt, ar, ai, *, bsz, nchunks):
    g, rows, _ = ut.shape
    blk = lambda s: pl.BlockSpec((1,) + s, lambda i: (i, 0, 0))
    return pl.pallas_call(
        functools.partial(_s5_apply_kernel, bsz=bsz, nchunks=nchunks),
        grid=(g,),
        in_specs=[blk((rows, S5_W)), blk((S5_TILE, S5_W)), blk((S5_W, 2 * S5_N)), blk((2 * S5_N, S5_W)),
                  blk((1, S5_N)), blk((1, S5_N))],
        out_specs=blk((rows, S5_W)),
        out_shape=jax.ShapeDtypeStruct((g, rows, S5_W), BF16),
        scratch_shapes=[pltpu.VMEM((rows, S5_N), F32)] * 4 + [pltpu.VMEM((rows, S5_W), F32)],
        compiler_params=_cparams("parallel"),
        name="s5_apply",
    )(ut, strip, wz, pt, ar, ai)


def _s5_glu_kernel(h_ref, y_ref, g_ref, d_ref, w_ref, b_ref, o_ref, yv_ref, yb_ref, *, tn):
    j = pl.program_id(1)

    @pl.when(j == 0)
    def _():
        g = g_ref[...]
        dv = d_ref[...]

        def body(b, carry):
            rs = pl.ds(pl.multiple_of(b * GLU_ROWS, GLU_ROWS), GLU_ROWS)
            y = y_ref[rs, :].astype(F32) + dv * _rms(h_ref[rs, :], g)
            y = 0.5 * y * (1.0 + jnp.tanh(math.sqrt(2.0 / math.pi) * (y + 0.044715 * (y * y * y))))
            yv_ref[rs, :] = y
            yb_ref[rs, :] = y.astype(BF16)
            return carry

        lax.fori_loop(0, h_ref.shape[0] // GLU_ROWS, body, 0)

    cols = pl.ds(pl.multiple_of(j * tn, tn), tn)
    z = _dot(yb_ref[...], w_ref[...]) + b_ref[...]
    o_ref[...] = h_ref[:, cols] + yv_ref[:, cols] * _sigmoid(z)


def _s5_glu(h, y, g, dvec, w_glu, o, b_glu, *, tm, tn):
    n, d = h.shape
    return pl.pallas_call(
        functools.partial(_s5_glu_kernel, tn=tn),
        grid=(n // tm, d // tn),
        in_specs=[
            pl.BlockSpec((tm, d), lambda i, j: (i, 0)),
            pl.BlockSpec((tm, d), lambda i, j: (i, 0)),
            pl.BlockSpec((1, d), lambda i, j: (0, 0)),
            pl.BlockSpec((1, d), lambda i, j: (0, 0)),
            pl.BlockSpec((None, d, tn), lambda i, j: (o, 0, j)),
            pl.BlockSpec((1, tn), lambda i, j: (0, j)),
        ],
        out_specs=pl.BlockSpec((tm, tn), lambda i, j: (i, j)),
        out_shape=jax.ShapeDtypeStruct((n, d), F32),
        scratch_shapes=[pltpu.VMEM((tm, d), F32), pltpu.VMEM((tm, d), BF16)],
        compiler_params=_cparams("parallel", "arbitrary"),
        name="s5_glu",
    )(h, y, g, dvec, w_glu, b_glu)


def _even_layer(h, g_mix, w_main, w_gate, e, m_conv_w, m_conv_b, m_b_igate, m_b_fgate, m_head_norm,
                g_w_gate, g_b_gate, g_head_norm, w_out, *, bsz, seq, tm, tb):
    n, d = h.shape
    pm, gates = _inproj(h, g_mix.reshape(1, d), w_main, w_gate, e, tm=tm, tn=1024)
    gates_t = gates[:, 0:2 * M_HEADS].reshape(bsz, seq // CHUNK, CHUNK, 2, M_HEADS).transpose(3, 0, 1, 4, 2)
    gates_t = gates_t.reshape(2, bsz, seq // CHUNK * (M_HEADS // 2), 2 * CHUNK)
    bias = jnp.repeat(jnp.stack([m_b_igate, m_b_fgate]).astype(F32), CHUNK, axis=-1)
    bias = bias.reshape(2, M_HEADS // 2, 2 * CHUNK)
    hm = _mlstm(pm, gates_t, m_conv_w, m_conv_b.reshape(1, -1), bias, m_head_norm.reshape(1, -1),
                bsz=bsz, seq=seq, tb=tb)
    wg_pad = jnp.zeros((GATE_LANES, G_HEADS * G_DK), BF16).at[16:16 + G_RANK].set(g_w_gate.astype(BF16))
    hg = _gla(pm, gates, wg_pad, g_b_gate.reshape(1, -1), g_head_norm.reshape(1, -1),
              bsz=bsz, seq=seq, tb=tb)
    return _outproj(hm, hg, w_out, e, h, tm=tm, tn=1024)


def _odd_layer(h, u, g_mix, lam_re, lam_im, log_dt, b_re, b_im, c_re, c_im, dvec, w_glu, o, b_glu,
               *, bsz, seq, tm):
    n, d = h.shape
    groups = d // S5_P
    nchunks = seq // CHUNK
    ops = _s5_gen(lam_re, lam_im, log_dt, jnp.swapaxes(b_re, 1, 2), jnp.swapaxes(b_im, 1, 2),
                  jnp.swapaxes(c_re, 1, 2), jnp.swapaxes(c_im, 1, 2))
    ut = u.reshape(bsz, nchunks, CHUNK, groups, S5_P).transpose(3, 1, 0, 2, 4)
    ut = ut.reshape(groups, nchunks * bsz, S5_W)
    yt = _s5_apply(ut, *ops, bsz=bsz, nchunks=nchunks)
    y = yt.reshape(groups, nchunks, bsz, CHUNK, S5_P).transpose(2, 1, 3, 0, 4).reshape(n, d)
    return _s5_glu(h, y, g_mix.reshape(1, d), dvec.reshape(1, d), w_glu, o, b_glu.reshape(1, d),
                   tm=tm, tn=1024)


def _regroup_kernel(w_ref, o_ref, *, segments):
    for src, dst, n in segments:
        if src is None:
            o_ref[:, dst:dst + n] = jnp.zeros((o_ref.shape[0], n), o_ref.dtype)
        else:
            o_ref[:, dst:dst + n] = w_ref[:, src:src + n].astype(o_ref.dtype)


def _regroup_cols(w, segments, width, dtype, *, rows):
    nl, r, c = w.shape
    return pl.pallas_call(
        functools.partial(_regroup_kernel, segments=segments),
        grid=(nl, r // rows),
        in_specs=[pl.BlockSpec((None, rows, c), lambda l, i: (l, i, 0))],
        out_specs=pl.BlockSpec((None, rows, width), lambda l, i: (l, i, 0)),
        out_shape=jax.ShapeDtypeStruct((nl, r, width), dtype),
        compiler_params=_cparams("parallel", "parallel"),
        name="regroup_cols",
    )(w)


def _tile_up_kernel(w_ref, o_ref, *, dff, tk):
    for k in range(o_ref.shape[0]):
        n = min(tk, dff - k * tk)
        for half, base in ((0, 0), (1, dff)):
            o_ref[k, half, :, 0:n] = w_ref[:, base + k * tk:base + k * tk + n].astype(o_ref.dtype)
            if n < tk:
                o_ref[k, half, :, n:tk] = jnp.zeros((o_ref.shape[2], tk - n), o_ref.dtype)


def _tile_up(w, tk, *, rows):
    nl, d, two_ff = w.shape
    dff = two_ff // 2
    nk = -(-dff // tk)
    return pl.pallas_call(
        functools.partial(_tile_up_kernel, dff=dff, tk=tk),
        grid=(nl, d // rows),
        in_specs=[pl.BlockSpec((None, rows, two_ff), lambda l, i: (l, i, 0))],
        out_specs=pl.BlockSpec((None, nk, 2, rows, tk), lambda l, i: (l, 0, 0, i, 0)),
        out_shape=jax.ShapeDtypeStruct((nl, nk, 2, d, tk), BF16),
        compiler_params=_cparams("parallel", "parallel"),
        name="tile_up",
    )(w)


def _prep_ffn(ffn_w_up, ffn_conv_w, ffn_conv_b, ffn_w_down, tk):
    dff = ffn_w_down.shape[1]
    pad = -dff % tk
    dffp = dff + pad

    def padded(a, dtype):
        lead = a.shape[:-1]
        gv = jnp.pad(a.reshape(lead + (2, dff)), [(0, 0)] * len(lead) + [(0, 0), (0, pad)])
        return gv.astype(dtype).reshape(lead + (2 * dffp,))

    wup = _tile_up(ffn_w_up, tk, rows=128)
    cw = padded(ffn_conv_w, F32)
    cb = padded(ffn_conv_b, F32)[:, None, :]
    wdn = jnp.pad(ffn_w_down, [(0, 0), (0, pad), (0, 0)]).astype(BF16)
    return wup, cw, cb, wdn


def kernel(x, norm_mix, norm_ffn, ffn_w_up, ffn_conv_w, ffn_conv_b, ffn_w_down, norm_final,
           w_in, m_conv_w, m_conv_b, m_b_igate, m_b_fgate, m_head_norm,
           g_w_gate, g_b_gate, g_head_norm, w_out,
           s5_lambda_re, s5_lambda_im, s5_log_dt, s5_b_re, s5_b_im, s5_c_re, s5_c_im,
           s5_d, s5_w_glu, s5_b_glu):
    return _forward(x, norm_mix, norm_ffn, ffn_w_up, ffn_conv_w, ffn_conv_b, ffn_w_down, norm_final,
                    w_in, m_conv_w, m_conv_b, m_b_igate, m_b_fgate, m_head_norm,
                    g_w_gate, g_b_gate, g_head_norm, w_out,
                    s5_lambda_re, s5_lambda_im, s5_log_dt, s5_b_re, s5_b_im, s5_c_re, s5_c_im,
                    s5_d, s5_w_glu, s5_b_glu, tm=512, tf=512, tb=512)


def _forward(x, norm_mix, norm_ffn, ffn_w_up, ffn_conv_w, ffn_conv_b, ffn_w_down, norm_final,
             w_in, m_conv_w, m_conv_b, m_b_igate, m_b_fgate, m_head_norm,
             g_w_gate, g_b_gate, g_head_norm, w_out,
             s5_lambda_re, s5_lambda_im, s5_log_dt, s5_b_re, s5_b_im, s5_c_re, s5_c_im,
             s5_d, s5_w_glu, s5_b_glu, *, tm, tf, tb):
    bsz, seq, d = x.shape
    depth = norm_mix.shape[0]
    n = bsz * seq
    tmd = min(2 * tm, n)
    h = x.reshape(n, d)

    wup, cw, cb, wdn = _prep_ffn(ffn_w_up, ffn_conv_w, ffn_conv_b, ffn_w_down, 512)
    c0 = 2 * M_HEADS * M_DK + 2 * M_HEADS * M_DV
    c1 = c0 + 2 * M_HEADS
    c2 = c1 + 2 * G_HEADS * G_DK + 2 * G_HEADS * G_DV
    ng = (c1 - c0) + G_RANK
    w_main = _regroup_cols(w_in, ((0, 0, c0), (c1, c0, c2 - c1)), c0 + c2 - c1, BF16, rows=256)
    w_gate = _regroup_cols(w_in, ((c0, 0, c1 - c0), (c2, c1 - c0, G_RANK), (None, ng, GATE_LANES - ng)),
                           GATE_LANES, BF16, rows=256)
    w_out_b = w_out.astype(BF16)
    w_glu_b = s5_w_glu.astype(BF16)

    for layer in range(depth):
        if layer % 2 == 0:
            e = layer // 2
            h = _even_layer(h, norm_mix[layer], w_main, w_gate, e, m_conv_w[e], m_conv_b[e],
                            m_b_igate[e], m_b_fgate[e], m_head_norm[e], g_w_gate[e], g_b_gate[e],
                            g_head_norm[e], w_out_b, bsz=bsz, seq=seq, tm=tmd, tb=tb)
        else:
            o = layer // 2
            h = _odd_layer(h, u, norm_mix[layer], s5_lambda_re[o], s5_lambda_im[o], s5_log_dt[o],
                           s5_b_re[o], s5_b_im[o], s5_c_re[o], s5_c_im[o], s5_d[o], w_glu_b, o,
                           s5_b_glu[o], bsz=bsz, seq=seq, tm=tm)
        if layer == depth - 1:
            post, gpost = "inplace", norm_final
        elif layer % 2 == 0:
            post, gpost = "extra", norm_mix[layer + 1]
        else:
            post, gpost = None, norm_ffn[layer]
        outs = _ffn(h, norm_ffn[layer].reshape(1, d), wup, cw, cb, wdn, layer, gpost.reshape(1, d), post,
                    seq=seq, tm=tf, tk=512, rb=min(256, tf))
        h = outs[0]
        u = outs[1] if post == "extra" else None
    return h.reshape(bsz, seq, d)
```
